```python
import math
import jax, jax.numpy as jnp
from jax import lax
import numpy as np

D_MODEL = 1024
BATCH = 2
SEQ = 8192
DEPTH = 1
DEC_BATCH = 128
DEC_SEQ = 4
PAST_LEN = 8192
PAGE_SIZE = 128

GDN_HEADS = D_MODEL // 256
GDN_DK = 128
GDN_DV = 128
CONV_W = 4
GDN_CHUNK = 64
DIFF_HEADS = D_MODEL // 256
DIFF_HD = 128
DIFF_HALF = DIFF_HD // 2
DIFF_VD = 128
Q_BLOCK = 128
N_EXPERTS = 32
TOP_K = 4
D_FF = D_MODEL
SWIGLU_LIMIT = 7.0
SWIGLU_ALPHA = 1.702
EPS = 1e-6

GDN_QK_W = GDN_HEADS * GDN_DK
GDN_V_W = GDN_HEADS * GDN_DV
CONV_DIM = 2 * GDN_QK_W + GDN_V_W
DIFF_QK_W = DIFF_HEADS * DIFF_HD
DIFF_V_W = DIFF_HEADS * DIFF_VD
D_MIX = GDN_V_W + DIFF_V_W
IN_SIZES = (CONV_DIM, GDN_V_W, GDN_HEADS, GDN_HEADS, DIFF_QK_W, DIFF_QK_W, DIFF_V_W)
D_IN = sum(IN_SIZES)

kernel_name = "hymba_gdn_diffattn_moe_step"


def rmsnorm(x, g):
    xf = x.astype(jnp.float32)
    xf = xf * lax.rsqrt(jnp.mean(xf * xf, axis=-1, keepdims=True) + EPS)
    return (xf * g.astype(jnp.float32)).astype(x.dtype)


def l2norm(x):
    return x * lax.rsqrt(jnp.sum(x * x, axis=-1, keepdims=True) + EPS)


def split_proj(p):
    idx = [int(i) for i in np.cumsum(IN_SIZES)[:-1]]
    return jnp.split(p, idx, axis=-1)


def short_conv(xp, w):
    L = xp.shape[1] - (CONV_W - 1)
    out = xp[:, 0:L] * w[0]
    for j in range(1, CONV_W):
        out = out + xp[:, j:j + L] * w[j]
    return jax.nn.silu(out)


def gdn_prepare(conv_out, a, b, a_log, dt_bias):
    B, L, _ = conv_out.shape
    c = conv_out.astype(jnp.float32)
    q, k, v = jnp.split(c, [GDN_QK_W, 2 * GDN_QK_W], axis=-1)
    q = l2norm(q.reshape(B, L, GDN_HEADS, GDN_DK)) * (GDN_DK ** -0.5)
    k = l2norm(k.reshape(B, L, GDN_HEADS, GDN_DK))
    v = v.reshape(B, L, GDN_HEADS, GDN_DV)
    g = -jnp.exp(a_log.astype(jnp.float32)) * jax.nn.softplus(a.astype(jnp.float32) + dt_bias.astype(jnp.float32))
    beta = jax.nn.sigmoid(b.astype(jnp.float32))
    q, k, v = jnp.moveaxis(q, 2, 1), jnp.moveaxis(k, 2, 1), jnp.moveaxis(v, 2, 1)
    return q, k, v, jnp.moveaxis(g, 2, 1), jnp.moveaxis(beta, 2, 1)


def gdn_chunk(S, q, k, v, g, beta):
    C = q.shape[2]
    gam = jnp.cumsum(g, axis=-1)
    incl = jnp.tril(jnp.ones((C, C), dtype=bool))
    strict = jnp.tril(jnp.ones((C, C), dtype=bool), k=-1)
    dec = jnp.exp(jnp.where(incl, gam[..., :, None] - gam[..., None, :], -jnp.inf))
    kk = jnp.einsum('bhid,bhjd->bhij', k, k)
    a_mat = jnp.where(strict, beta[..., :, None] * kk * dec, 0.0) + jnp.eye(C, dtype=k.dtype)
    rhs = jnp.concatenate([beta[..., None] * v, beta[..., None] * k * jnp.exp(gam)[..., None]], axis=-1)
    sol = lax.linalg.triangular_solve(a_mat, rhs, left_side=True, lower=True, unit_diagonal=True)
    u, w = sol[..., :GDN_DV], sol[..., GDN_DV:]
    delta = u - jnp.einsum('bhid,bhdv->bhiv', w, S)
    qk = jnp.einsum('bhid,bhjd->bhij', q, k) * dec
    o = jnp.exp(gam)[..., None] * jnp.einsum('bhid,bhdv->bhiv', q, S) + jnp.einsum('bhij,bhjv->bhiv', qk, delta)
    tail = jnp.exp(gam[..., -1:] - gam)
    S_new = jnp.exp(gam[..., -1])[..., None, None] * S + jnp.einsum('bhjd,bhjv->bhdv', k * tail[..., None], delta)
    return o, S_new


def gdn_prompt_scan(q, k, v, g, beta):
    B, H, L, _ = q.shape
    n = L // GDN_CHUNK

    def to_chunks(t):
        return jnp.moveaxis(t.reshape(t.shape[:2] + (n, GDN_CHUNK) + t.shape[3:]), 2, 0)

    def step(S, xs):
        o, S = gdn_chunk(S, *xs)
        return S, o

    S0 = jnp.zeros((B, H, GDN_DK, GDN_DV), jnp.float32)
    S_fin, o = lax.scan(step, S0, (to_chunks(q), to_chunks(k), to_chunks(v), to_chunks(g), to_chunks(beta)))
    o = jnp.moveaxis(o, 0, 2).reshape(B, H, L, GDN_DV)
    return o, S_fin


def gdn_output(o, z, norm_g):
    B, H, L, _ = o.shape
    o = jnp.moveaxis(o, 1, 2)
    zf = z.astype(jnp.float32).reshape(B, L, GDN_HEADS, GDN_DV)
    out = rmsnorm(o, norm_g) * jax.nn.silu(zf)
    return out.reshape(B, L, GDN_V_W).astype(z.dtype)


def diff_lambda(lq1, lk1, lq2, lk2, lam_init):
    s1 = jnp.sum(lq1.astype(jnp.float32) * lk1.astype(jnp.float32))
    s2 = jnp.sum(lq2.astype(jnp.float32) * lk2.astype(jnp.float32))
    return jnp.exp(s1) - jnp.exp(s2) + lam_init


def diff_probs(s1, s2, mask, lam):
    s1 = jnp.where(mask, s1.astype(jnp.float32), -jnp.inf)
    s2 = jnp.where(mask, s2.astype(jnp.float32), -jnp.inf)
    return jax.nn.softmax(s1, axis=-1) - lam * jax.nn.softmax(s2, axis=-1)


def diff_attn_prompt(q, k, v, lam):
    B, L, H, _ = q.shape
    q = q * (DIFF_HALF ** -0.5)
    nb = L // Q_BLOCK
    qb = jnp.moveaxis(q.reshape(B, nb, Q_BLOCK, H, DIFF_HD), 1, 0)
    k1, k2 = k[..., :DIFF_HALF], k[..., DIFF_HALF:]
    kpos = jnp.arange(L)

    def block(args):
        i, qi = args
        qpos = i * Q_BLOCK + jnp.arange(Q_BLOCK)
        mask = kpos[None, :] <= qpos[:, None]
        s1 = jnp.einsum('bqhd,bkhd->bhqk', qi[..., :DIFF_HALF], k1)
        s2 = jnp.einsum('bqhd,bkhd->bhqk', qi[..., DIFF_HALF:], k2)
        p = diff_probs(s1, s2, mask, lam).astype(v.dtype)
        return jnp.einsum('bhqk,bkhd->bqhd', p, v)

    out = lax.map(block, (jnp.arange(nb), qb))
    return jnp.moveaxis(out, 0, 1).reshape(B, L, H, DIFF_VD)


def diff_attn_sample(q, k_new, v_new, k_past, v_past, lam):
    T = q.shape[1]
    P = k_past.shape[1]
    q = q * (DIFF_HALF ** -0.5)

    def scores(lo, hi):
        sp = jnp.einsum('bqhd,bkhd->bhqk', q[..., lo:hi], k_past[..., lo:hi])
        sn = jnp.einsum('bqhd,bkhd->bhqk', q[..., lo:hi], k_new[..., lo:hi])
        return jnp.concatenate([sp, sn], axis=-1)

    mask = jnp.concatenate([jnp.ones((T, P), dtype=bool), jnp.tril(jnp.ones((T, T), dtype=bool))], axis=-1)
    p = diff_probs(scores(0, DIFF_HALF), scores(DIFF_HALF, DIFF_HD), mask, lam).astype(v_new.dtype)
    return jnp.einsum('bhqk,bkhd->bqhd', p[..., :P], v_past) + jnp.einsum('bhqk,bkhd->bqhd', p[..., P:], v_new)


def diff_output(o, norm_g, lam_init):
    B, L = o.shape[:2]
    return (rmsnorm(o, norm_g) * (1.0 - lam_init)).reshape(B, L, DIFF_V_W)


def moe(h, w_router, b_router, w_gate, b_gate, w_up, b_up, w_down, b_down):
    shape = h.shape
    t = h.reshape(-1, shape[-1])
    logits = (t @ w_router + b_router).astype(jnp.float32)
    top_v, top_i = lax.top_k(logits, TOP_K)
    gates = jax.nn.softmax(top_v, axis=-1)
    comb = jnp.sum(jax.nn.one_hot(top_i, N_EXPERTS, dtype=jnp.float32) * gates[..., None], axis=1).astype(h.dtype)
    out = jnp.zeros_like(t)
    for e in range(N_EXPERTS):
        gl = jnp.minimum(t @ w_gate[e] + b_gate[e], SWIGLU_LIMIT)
        lin = jnp.clip(t @ w_up[e] + b_up[e], -SWIGLU_LIMIT, SWIGLU_LIMIT)
        act = gl * jax.nn.sigmoid(SWIGLU_ALPHA * gl) * (lin + 1.0)
        out = out + comb[:, e:e + 1] * (act @ w_down[e] + b_down[e])
    return out.reshape(shape)


def setup_inputs(seed: int = 0) -> dict:
    key = jax.random.key(seed)
    ks = jax.random.split(key, 40)
    f32 = jnp.float32

    def nrm(k, shape, s):
        return jax.random.normal(k, shape, f32) * s

    n_pages = PAST_LEN // PAGE_SIZE
    n_used = DEC_BATCH * n_pages
    n_pool = n_used + n_used // 4
    page_table = jax.random.permutation(ks[0], n_pool)[:n_used].reshape(DEC_BATCH, n_pages).astype(jnp.int32)
    dt = jnp.exp(jax.random.uniform(ks[1], (DEPTH, GDN_HEADS), f32, math.log(1e-3), math.log(1e-1)))
    dt_bias = dt + jnp.log(-jnp.expm1(-dt))
    return {
        "x_prompt": nrm(ks[2], (BATCH, SEQ, D_MODEL), 1.0),
        "x_sample": nrm(ks[3], (DEC_BATCH, DEC_SEQ, D_MODEL), 1.0),
        "cache_k": nrm(ks[4], (DEPTH, n_pool, PAGE_SIZE, DIFF_HEADS, DIFF_HD), 1.0),
        "cache_v": nrm(ks[5], (DEPTH, n_pool, PAGE_SIZE, DIFF_HEADS, DIFF_VD), 1.0),
        "state_gdn": nrm(ks[6], (DEPTH, DEC_BATCH, GDN_HEADS, GDN_DK, GDN_DV), 0.1),
        "state_conv": nrm(ks[7], (DEPTH, DEC_BATCH, CONV_W - 1, CONV_DIM), 1.0),
        "page_table": page_table,
        "ln1_g": 1.0 + nrm(ks[8], (DEPTH, D_MODEL), 0.1),
        "w_in": nrm(ks[9], (DEPTH, D_MODEL, D_IN), D_MODEL ** -0.5),
        "conv_w": nrm(ks[10], (DEPTH, CONV_W, CONV_DIM), CONV_W ** -0.5),
        "a_log": jnp.log(jax.random.uniform(ks[11], (DEPTH, GDN_HEADS), f32, 1.0, 16.0)),
        "dt_bias": dt_bias,
        "gdn_norm_g": 1.0 + nrm(ks[12], (DEPTH, GDN_DV), 0.1),
        "lambda_q1": nrm(ks[13], (DEPTH, DIFF_HALF), 0.1),
        "lambda_k1": nrm(ks[14], (DEPTH, DIFF_HALF), 0.1),
        "lambda_q2": nrm(ks[15], (DEPTH, DIFF_HALF), 0.1),
        "lambda_k2": nrm(ks[16], (DEPTH, DIFF_HALF), 0.1),
        "diff_norm_g": 1.0 + nrm(ks[17], (DEPTH, DIFF_VD), 0.1),
        "w_o": nrm(ks[18], (DEPTH, D_MIX, D_MODEL), D_MIX ** -0.5),
        "ln2_g": 1.0 + nrm(ks[19], (DEPTH, D_MODEL), 0.1),
        "w_router": nrm(ks[20], (DEPTH, D_MODEL, N_EXPERTS), D_MODEL ** -0.5),
        "b_router": nrm(ks[21], (DEPTH, N_EXPERTS), 0.01),
        "w_gate": nrm(ks[22], (DEPTH, N_EXPERTS, D_MODEL, D_FF), D_MODEL ** -0.5),
        "b_gate": nrm(ks[23], (DEPTH, N_EXPERTS, D_FF), 0.02),
        "w_up": nrm(ks[24], (DEPTH, N_EXPERTS, D_MODEL, D_FF), D_MODEL ** -0.5),
        "b_up": nrm(ks[25], (DEPTH, N_EXPERTS, D_FF), 0.02),
        "w_down": nrm(ks[26], (DEPTH, N_EXPERTS, D_FF, D_MODEL), D_FF ** -0.5),
        "b_down": nrm(ks[27], (DEPTH, N_EXPERTS, D_MODEL), 0.02),
        "final_g": 1.0 + nrm(ks[28], (D_MODEL,), 0.1),
    }


def reference(x_prompt, x_sample, cache_k, cache_v, state_gdn, state_conv, page_table,
              ln1_g, w_in, conv_w, a_log, dt_bias, gdn_norm_g,
              lambda_q1, lambda_k1, lambda_q2, lambda_k2, diff_norm_g, w_o,
              ln2_g, w_router, b_router, w_gate, b_gate, w_up, b_up, w_down, b_down, final_g):
    B, L, _ = x_prompt.shape
    DB, T, _ = x_sample.shape
    past = page_table.shape[1] * cache_k.shape[2]
    hp, hs = x_prompt, x_sample
    kp_l, vp_l, ks_l, vs_l, sp_l, cp_l, ss_l, cs_l = [], [], [], [], [], [], [], []
    for l in range(DEPTH):
        lam_init = 0.8 - 0.6 * math.exp(-0.3 * l)
        lam = diff_lambda(lambda_q1[l], lambda_k1[l], lambda_q2[l], lambda_k2[l], lam_init)
        moe_w = (w_router[l], b_router[l], w_gate[l], b_gate[l], w_up[l], b_up[l], w_down[l], b_down[l])

        qkv_p, z_p, a_p, b_p, qd_p, kd_p, vd_p = split_proj(rmsnorm(hp, ln1_g[l]) @ w_in[l])
        xc_p = jnp.concatenate([jnp.zeros((B, CONV_W - 1, CONV_DIM), qkv_p.dtype), qkv_p], axis=1)
        q, k, v, g, beta = gdn_prepare(short_conv(xc_p, conv_w[l]), a_p, b_p, a_log[l], dt_bias[l])
        o_g, S_p = gdn_prompt_scan(q, k, v, g, beta)
        g_out = gdn_output(o_g, z_p, gdn_norm_g[l])
        kd_p = kd_p.reshape(B, L, DIFF_HEADS, DIFF_HD)
        vd_p = vd_p.reshape(B, L, DIFF_HEADS, DIFF_VD)
        o_d = diff_attn_prompt(qd_p.reshape(B, L, DIFF_HEADS, DIFF_HD), kd_p, vd_p, lam)
        d_out = diff_output(o_d, diff_norm_g[l], lam_init)
        hp = hp + jnp.concatenate([g_out, d_out], axis=-1) @ w_o[l]
        hp = hp + moe(rmsnorm(hp, ln2_g[l]), *moe_w)

        qkv_s, z_s, a_s, b_s, qd_s, kd_s, vd_s = split_proj(rmsnorm(hs, ln1_g[l]) @ w_in[l])
        xc_s = jnp.concatenate([state_conv[l].astype(qkv_s.dtype), qkv_s], axis=1)
        q, k, v, g, beta = gdn_prepare(short_conv(xc_s, conv_w[l]), a_s, b_s, a_log[l], dt_bias[l])
        o_g, S_s = gdn_chunk(state_gdn[l].astype(jnp.float32), q, k, v, g, beta)
        g_out = gdn_output(o_g, z_s, gdn_norm_g[l])
        kd_s = kd_s.reshape(DB, T, DIFF_HEADS, DIFF_HD)
        vd_s = vd_s.reshape(DB, T, DIFF_HEADS, DIFF_VD)
        k_past = cache_k[l, page_table].reshape(DB, past, DIFF_HEADS, DIFF_HD).astype(kd_s.dtype)
        v_past = cache_v[l, page_table].reshape(DB, past, DIFF_HEADS, DIFF_VD).astype(vd_s.dtype)
        o_d = diff_attn_sample(qd_s.reshape(DB, T, DIFF_HEADS, DIFF_HD), kd_s, vd_s, k_past, v_past, lam)
        d_out = diff_output(o_d, diff_norm_g[l], lam_init)
        hs = hs + jnp.concatenate([g_out, d_out], axis=-1) @ w_o[l]
        hs = hs + moe(rmsnorm(hs, ln2_g[l]), *moe_w)

        kp_l.append(kd_p.astype(cache_k.dtype))
        vp_l.append(vd_p.astype(cache_v.dtype))
        ks_l.append(kd_s.astype(cache_k.dtype))
        vs_l.append(vd_s.astype(cache_v.dtype))
        sp_l.append(S_p.astype(state_gdn.dtype))
        cp_l.append(qkv_p[:, L - (CONV_W - 1):].astype(state_conv.dtype))
        ss_l.append(S_s.astype(state_gdn.dtype))
        cs_l.append(xc_s[:, xc_s.shape[1] - (CONV_W - 1):].astype(state_conv.dtype))

    y_prompt = rmsnorm(hp, final_g)
    y_sample = rmsnorm(hs, final_g)
    k_prompt = jnp.stack(kp_l, axis=0)
    v_prompt = jnp.stack(vp_l, axis=0)
    k_sample = jnp.stack(ks_l, axis=0)
    v_sample = jnp.stack(vs_l, axis=0)
    gdn_state_prompt = jnp.stack(sp_l, axis=0)
    conv_state_prompt = jnp.stack(cp_l, axis=0)
    gdn_state_sample = jnp.stack(ss_l, axis=0)
    conv_state_sample = jnp.stack(cs_l, axis=0)
    return (y_prompt, y_sample, k_prompt, v_prompt, k_sample, v_sample,
            gdn_state_prompt, conv_state_prompt, gdn_state_sample, conv_state_sample)
```

```python
import functools
import math

import jax
import jax.numpy as jnp
from jax import lax
from jax.experimental import pallas as pl
from jax.experimental.pallas import tpu as pltpu

F32 = jnp.float32
BF16 = jnp.bfloat16
I32 = jnp.int32

D_MODEL = 1024
HEAD_W = 128
N_HEADS = 4
GROUP_W = N_HEADS * HEAD_W
CONV_W = 4
CONV_DIM = 3 * GROUP_W
GDN_CHUNK = 64
DIFF_HALF = HEAD_W // 2
PAGE = 128
N_EXPERTS = 32
TOP_K = 4
SWIGLU_LIMIT = 7.0
SWIGLU_ALPHA = 1.702
EPS = 1e-6
LANES = 128
HIST_ROWS = 8
D_IN_PAD = CONV_DIM + 4 * GROUP_W + LANES
VMEM_LIMIT = 56 * 1024 * 1024


def _pick_tile(n, pref, mult=16):
    t = min(pref, n)
    while t > mult and (n % t or t % mult):
        t -= mult
    assert n % t == 0, (n, pref)
    return t


def _dot(a, b):
    return jnp.dot(a.astype(BF16), b.astype(BF16), preferred_element_type=F32)


def _dot_nt(a, b):
    return lax.dot_general(a.astype(BF16), b.astype(BF16), (((1,), (1,)), ((), ())),
                           preferred_element_type=F32)


def _dot_tn(a, b):
    return lax.dot_general(a.astype(BF16), b.astype(BF16), (((0,), (0,)), ((), ())),
                           preferred_element_type=F32)


def _split2(x):
    hi = x.astype(BF16)
    lo = (x - hi.astype(F32)).astype(BF16)
    return hi, lo


def _dot3(a, b, dot=_dot):
    ah, al = _split2(a)
    bh, bl = _split2(b)
    return dot(ah, bh) + (dot(ah, bl) + dot(al, bh))


def _sigmoid(x):
    return 1.0 / (1.0 + jnp.exp(-x))


def _inproj_body(x_ref, g_ref, w_ref, qkv_ref, z_ref, kd_ref, vd_ref, ab_ref,
                 qdb_ref, kdb_ref, vdb_ref):
    x = x_ref[...]
    ms = jnp.mean(x * x, axis=-1, keepdims=True)
    xn = (x * lax.rsqrt(ms + EPS) * g_ref[...]).astype(BF16)

    def seg(lo, hi):
        return jnp.dot(xn, w_ref[:, lo:hi], preferred_element_type=F32)

    o = CONV_DIM
    qkv_ref[...] = seg(0, o)
    z_ref[...] = seg(o, o + GROUP_W)
    qd = seg(o + GROUP_W, o + 2 * GROUP_W)
    qdb_ref[...] = (qd * (DIFF_HALF ** -0.5)).astype(BF16)
    kd = seg(o + 2 * GROUP_W, o + 3 * GROUP_W)
    kd_ref[...] = kd
    kdb_ref[...] = kd.astype(BF16)
    vd = seg(o + 3 * GROUP_W, o + 4 * GROUP_W)
    vd_ref[...] = vd
    vdb_ref[...] = vd.astype(BF16)
    ab_ref[...] = seg(o + 4 * GROUP_W, o + 4 * GROUP_W + LANES)


def _inproj(x, ln_g, w_cat):
    n = x.shape[0]
    tm = _pick_tile(n, 512)
    row = lambda w: pl.BlockSpec((tm, w), lambda i: (i, 0))
    full = lambda a: pl.BlockSpec(a.shape, lambda i: (0,) * a.ndim)
    outs = [(CONV_DIM, F32), (GROUP_W, F32), (GROUP_W, F32), (GROUP_W, F32), (LANES, F32),
            (GROUP_W, BF16), (GROUP_W, BF16), (GROUP_W, BF16)]
    return pl.pallas_call(
        _inproj_body,
        grid=(n // tm,),
        in_specs=[row(D_MODEL), full(ln_g), full(w_cat)],
        out_specs=[row(w) for w, _ in outs],
        out_shape=[jax.ShapeDtypeStruct((n, w), dt) for w, dt in outs],
        compiler_params=pltpu.CompilerParams(
            dimension_semantics=("parallel",), vmem_limit_bytes=VMEM_LIMIT),
        name="inproj",
    )(x, ln_g, w_cat)


def _unit_lower_inverse(a, c):
    r = lax.broadcasted_iota(I32, (c, c), 0)
    q = lax.broadcasted_iota(I32, (c, c), 1)
    t = jnp.where(r == q, 1.0, 0.0).astype(F32) - a
    p = a
    n = 1
    while 2 * n < c:
        p = _dot3(p, p)
        t = t + _dot3(t, p)
        n *= 2
    return t


def _gdn_body(qkv_ref, ab_ref, z_ref, hist0_ref, s0_ref, convw_ref, gpar_ref, ng_ref,
              out_ref, sfin_ref, s_scr, hist_scr, *, chunk, cin, valid):
    c = pl.program_id(1)
    C = chunk

    @pl.when(c == 0)
    def _init():
        s_scr[...] = s0_ref[0]
        hist_scr[...] = hist0_ref[0]

    def pad_rows(t):
        if cin == C:
            return t
        return jnp.concatenate([t, jnp.zeros((C - cin, t.shape[1]), t.dtype)], axis=0)

    x = pad_rows(qkv_ref[0])
    xp = jnp.concatenate([hist_scr[...], x], axis=0)
    w = convw_ref[...]
    h0 = HIST_ROWS - (CONV_W - 1)
    conv = xp[h0:h0 + C] * w[0:1]
    for j in range(1, CONV_W):
        conv = conv + xp[h0 + j:h0 + j + C] * w[j:j + 1]
    hist_scr[...] = x[C - HIST_ROWS:C]
    act = conv * _sigmoid(conv)

    t = pad_rows(ab_ref[0])
    sp_in = t + gpar_ref[1:2, :]
    softplus = jnp.maximum(sp_in, 0.0) + jnp.log(1.0 + jnp.exp(-jnp.abs(sp_in)))
    gval = -jnp.exp(gpar_ref[0:1, :]) * softplus
    bval = _sigmoid(t)
    lane = lax.broadcasted_iota(I32, (C, LANES), 1)
    rowi = lax.broadcasted_iota(I32, (C, LANES), 0)
    gb = jnp.where(lane < N_HEADS, gval, bval)
    if valid < C:
        gb = jnp.where(rowi < valid, gb, 0.0)

    r = lax.broadcasted_iota(I32, (C, C), 0)
    q_ = lax.broadcasted_iota(I32, (C, C), 1)
    incl = r >= q_
    strict = r > q_
    tril = jnp.where(incl, 1.0, 0.0).astype(BF16)
    g_hi = gb.astype(BF16)
    g_r1 = gb - g_hi.astype(F32)
    g_mid = g_r1.astype(BF16)
    g_lo = (g_r1 - g_mid.astype(F32)).astype(BF16)
    gam = (jnp.dot(tril, g_hi, preferred_element_type=F32)
           + jnp.dot(tril, g_mid, preferred_element_type=F32)
           + jnp.dot(tril, g_lo, preferred_element_type=F32))
    gam_t = gam.T

    z = z_ref[0]
    for h in range(N_HEADS):
        q = act[:, h * HEAD_W:(h + 1) * HEAD_W]
        k = act[:, GROUP_W + h * HEAD_W:GROUP_W + (h + 1) * HEAD_W]
        v = act[:, 2 * GROUP_W + h * HEAD_W:2 * GROUP_W + (h + 1) * HEAD_W]
        q = q * lax.rsqrt(jnp.sum(q * q, axis=-1, keepdims=True) + EPS) * (HEAD_W ** -0.5)
        k = k * lax.rsqrt(jnp.sum(k * k, axis=-1, keepdims=True) + EPS)
        gcol = jnp.broadcast_to(gam[:, h:h + 1], (C, HEAD_W))
        bcol = jnp.broadcast_to(gb[:, N_HEADS + h:N_HEADS + h + 1], (C, HEAD_W))
        grow = jnp.broadcast_to(gam_t[h:h + 1, :], (C, C))
        dec = jnp.where(incl, jnp.exp(jnp.minimum(gcol[:, :C] - grow, 0.0)), 0.0)
        eg = jnp.exp(gcol)
        kk = _dot_nt(k, k)
        a_mat = jnp.where(strict, bcol[:, :C] * kk * dec, 0.0)
        t_inv = _unit_lower_inverse(a_mat, C)
        rhs = jnp.concatenate([bcol * v, bcol * k * eg], axis=1)
        sol = _dot3(t_inv, rhs)
        u, w_ = sol[:, :HEAD_W], sol[:, HEAD_W:]
        s_old = s_scr[h]
        wq_s = _dot3(jnp.concatenate([w_, q], axis=0), s_old)
        delta = u - wq_s[:C]
        qk = _dot_nt(q, k) * dec
        o = eg * wq_s[C:] + _dot3(qk, delta)
        glast = gam[C - 1:C, h:h + 1]
        tail = jnp.exp(jnp.broadcast_to(glast, (C, HEAD_W)) - gcol)
        s_new = (jnp.broadcast_to(jnp.exp(glast), (HEAD_W, HEAD_W)) * s_old
                 + _dot3(k * tail, delta, dot=_dot_tn))
        s_scr[h] = s_new
        on = o * lax.rsqrt(jnp.mean(o * o, axis=-1, keepdims=True) + EPS) * ng_ref[...]
        zh = z[:, h * HEAD_W:(h + 1) * HEAD_W]
        out_ref[0, :, h * HEAD_W:(h + 1) * HEAD_W] = on[:cin] * (zh * _sigmoid(zh))

    @pl.when(c == pl.num_programs(1) - 1)
    def _fin():
        sfin_ref[0] = s_scr[...]


def _gdn(qkv, ab, z, hist0, s0, conv_w, gpar, norm_g, *, valid):
    b, l, _ = qkv.shape
    cin = min(l, GDN_CHUNK)
    assert l % cin == 0 and (l == cin or valid == cin)
    nc = l // cin
    tok = lambda w: pl.BlockSpec((1, cin, w), lambda i, c: (i, c, 0))
    per_b = lambda a: pl.BlockSpec((1,) + a.shape[1:], lambda i, c: (i,) + (0,) * (a.ndim - 1))
    full = lambda a: pl.BlockSpec(a.shape, lambda i, c: (0,) * a.ndim)
    return pl.pallas_call(
        functools.partial(_gdn_body, chunk=GDN_CHUNK, cin=cin, valid=valid),
        grid=(b, nc),
        in_specs=[tok(CONV_DIM), tok(LANES), tok(GROUP_W), per_b(hist0), per_b(s0),
                  full(conv_w), full(gpar), full(norm_g)],
        out_specs=[tok(GROUP_W), per_b(s0)],
        out_shape=[jax.ShapeDtypeStruct((b, l, GROUP_W), F32),
                   jax.ShapeDtypeStruct(s0.shape, F32)],
        scratch_shapes=[pltpu.VMEM((N_HEADS, HEAD_W, HEAD_W), F32),
                        pltpu.VMEM((HIST_ROWS, CONV_DIM), F32)],
        compiler_params=pltpu.CompilerParams(
            dimension_semantics=("parallel", "arbitrary"), vmem_limit_bytes=VMEM_LIMIT),
        name="gdn",
    )(qkv, ab, z, hist0, s0, conv_w, gpar, norm_g)


def _stack_halves(q):
    lane = lax.broadcasted_iota(I32, q.shape, 1)
    zero = jnp.zeros_like(q)
    return jnp.concatenate([jnp.where(lane < DIFF_HALF, q, zero),
                            jnp.where(lane >= DIFF_HALF, q, zero)], axis=0)


def _rep_lanes(x, n):
    return x if n == 1 else jnp.concatenate([x] * n, axis=1)


def _attn_prompt_body(lam_ref, q_ref, k_ref, v_ref, ng_ref, o_ref, m_scr, l_scr, acc_scr,
                      *, tq, out_scale):
    qi = pl.program_id(2)
    qq = _stack_halves(q_ref[...])
    m_scr[...] = jnp.full(m_scr.shape, -jnp.inf, F32)
    l_scr[...] = jnp.zeros(l_scr.shape, F32)
    acc_scr[...] = jnp.zeros(acc_scr.shape, F32)
    nrep = tq // LANES

    def block(kb, masked):
        start = pl.multiple_of(kb * tq, tq)
        k = k_ref[pl.ds(start, tq), :]
        v = v_ref[pl.ds(start, tq), :]
        s = lax.dot_general(qq, k, (((1,), (1,)), ((), ())), preferred_element_type=F32)
        if masked:
            r = lax.broadcasted_iota(I32, s.shape, 0)
            r = jnp.where(r >= tq, r - tq, r)
            cidx = lax.broadcasted_iota(I32, s.shape, 1)
            s = jnp.where(cidx <= r, s, -jnp.inf)
        m_prev = m_scr[...]
        m_new = jnp.maximum(m_prev, jnp.max(s, axis=1, keepdims=True))
        alpha = jnp.exp(m_prev - m_new)
        p = jnp.exp(s - _rep_lanes(m_new, nrep))
        l_scr[...] = alpha * l_scr[...] + jnp.sum(p, axis=1, keepdims=True)
        acc_scr[...] = alpha * acc_scr[...] + jnp.dot(p.astype(BF16), v,
                                                      preferred_element_type=F32)
        m_scr[...] = m_new

    def body(kb, carry):
        block(kb, False)
        return carry

    lax.fori_loop(0, qi, body, 0)
    block(qi, True)

    o = acc_scr[...] / l_scr[...]
    o = o[:tq] - lam_ref[0, 0] * o[tq:]
    o = o * lax.rsqrt(jnp.mean(o * o, axis=-1, keepdims=True) + EPS) * ng_ref[...]
    o_ref[...] = o * out_scale


def _attn_prompt(lam, qdb, kdb, vdb, norm_g, *, batch, seq, out_scale):
    tq = _pick_tile(seq, 256, LANES)
    nq = seq // tq
    return pl.pallas_call(
        functools.partial(_attn_prompt_body, tq=tq, out_scale=out_scale),
        grid=(batch, N_HEADS, nq),
        in_specs=[pl.BlockSpec(memory_space=pltpu.SMEM),
                  pl.BlockSpec((tq, HEAD_W), lambda b, h, i: (b * nq + i, h)),
                  pl.BlockSpec((seq, HEAD_W), lambda b, h, i: (b, h)),
                  pl.BlockSpec((seq, HEAD_W), lambda b, h, i: (b, h)),
                  pl.BlockSpec((1, HEAD_W), lambda b, h, i: (0, 0))],
        out_specs=pl.BlockSpec((tq, HEAD_W), lambda b, h, i: (b * nq + i, h)),
        out_shape=jax.ShapeDtypeStruct((batch * seq, GROUP_W), F32),
        scratch_shapes=[pltpu.VMEM((2 * tq, LANES), F32), pltpu.VMEM((2 * tq, LANES), F32),
                        pltpu.VMEM((2 * tq, HEAD_W), F32)],
        compiler_params=pltpu.CompilerParams(
            dimension_semantics=("parallel", "parallel", "arbitrary"),
            vmem_limit_bytes=VMEM_LIMIT),
        name="attn_prompt",
    )(lam, qdb, kdb, vdb, norm_g)


Q_ROWS = 8
QBD_ROWS = 2 * N_HEADS * Q_ROWS


def _attn_decode_body(pt_ref, lam_ref, qbd_ref, kn_ref, vn_ref, ng_ref, *rest,
                      pages_per_step, n_new, out_scale):
    G = pages_per_step
    k_refs, v_refs = rest[:G], rest[G:2 * G]
    o_ref, m_scr, l_scr, acc_scr = rest[2 * G:]
    j = pl.program_id(1)

    @pl.when(j == 0)
    def _init():
        m_scr[...] = jnp.full(m_scr.shape, -jnp.inf, F32)
        l_scr[...] = jnp.zeros(l_scr.shape, F32)
        acc_scr[...] = jnp.zeros(acc_scr.shape, F32)

    qbd = qbd_ref[0]

    def update(s, vs):
        n = len(vs)
        m_prev = m_scr[...]
        m_new = jnp.maximum(m_prev, jnp.max(s, axis=1, keepdims=True))
        alpha = jnp.exp(m_prev - m_new)
        p = jnp.exp(s - _rep_lanes(m_new, n)).astype(BF16)
        l_scr[...] = alpha * l_scr[...] + jnp.sum(p.astype(F32), axis=1, keepdims=True)
        pv = jnp.dot(p[:, :LANES], vs[0], preferred_element_type=F32)
        for g in range(1, n):
            pv = pv + jnp.dot(p[:, g * LANES:(g + 1) * LANES], vs[g],
                              preferred_element_type=F32)
        acc_scr[...] = _rep_lanes(alpha, GROUP_W // LANES) * acc_scr[...] + pv
        m_scr[...] = m_new

    s = jnp.concatenate([_dot_nt(qbd, k_refs[g][...]) for g in range(G)], axis=1)
    update(s, [v_refs[g][...].astype(BF16) for g in range(G)])

    @pl.when(j == pl.num_programs(1) - 1)
    def _fin():
        pad = jnp.zeros((PAGE - Q_ROWS, GROUP_W), F32)
        kn = jnp.concatenate([kn_ref[0], pad], axis=0)
        vn = jnp.concatenate([vn_ref[0], pad], axis=0).astype(BF16)
        sn = _dot_nt(qbd, kn)
        t = lax.broadcasted_iota(I32, sn.shape, 0) % Q_ROWS
        cidx = lax.broadcasted_iota(I32, sn.shape, 1)
        sn = jnp.where((cidx <= t) & (cidx < n_new), sn, -jnp.inf)
        update(sn, [vn])
        acc = acc_scr[...]
        l = l_scr[...]
        lam = lam_ref[0, 0]
        half = N_HEADS * Q_ROWS
        for h in range(N_HEADS):
            rows = slice(h * Q_ROWS, (h + 1) * Q_ROWS)
            rows2 = slice(half + h * Q_ROWS, half + (h + 1) * Q_ROWS)
            cols = slice(h * HEAD_W, (h + 1) * HEAD_W)
            o = acc[rows, cols] / l[rows] - lam * (acc[rows2, cols] / l[rows2])
            o = o * lax.rsqrt(jnp.mean(o * o, axis=-1, keepdims=True) + EPS) * ng_ref[...]
            o_ref[0, :, cols] = o * out_scale


def _attn_decode(page_table, lam, qbd, kn, vn, norm_g, cache_k, cache_v, *, n_new, out_scale):
    db, n_pages = page_table.shape
    G = 8
    while n_pages % G:
        G //= 2
    nj = n_pages // G
    pt_flat = page_table.reshape(-1)

    def page_map(g):
        return lambda b, j, pt: (pt[b * n_pages + j * G + g], 0, 0)

    per_b = lambda a: pl.BlockSpec((1,) + a.shape[1:], lambda b, j, pt: (b, 0, 0))
    page_spec = lambda g: pl.BlockSpec((None, PAGE, GROUP_W), page_map(g))
    grid_spec = pltpu.PrefetchScalarGridSpec(
        num_scalar_prefetch=1,
        grid=(db, nj),
        in_specs=[pl.BlockSpec(memory_space=pltpu.SMEM), per_b(qbd), per_b(kn), per_b(vn),
                  pl.BlockSpec((1, HEAD_W), lambda b, j, pt: (0, 0))]
                 + [page_spec(g) for g in range(G)] + [page_spec(g) for g in range(G)],
        out_specs=pl.BlockSpec((1, Q_ROWS, GROUP_W), lambda b, j, pt: (b, 0, 0)),
        scratch_shapes=[pltpu.VMEM((QBD_ROWS, LANES), F32), pltpu.VMEM((QBD_ROWS, LANES), F32),
                        pltpu.VMEM((QBD_ROWS, GROUP_W), F32)],
    )
    return pl.pallas_call(
        functools.partial(_attn_decode_body, pages_per_step=G, n_new=n_new,
                          out_scale=out_scale),
        grid_spec=grid_spec,
        out_shape=jax.ShapeDtypeStruct((db, Q_ROWS, GROUP_W), F32),
        compiler_params=pltpu.CompilerParams(
            dimension_semantics=("parallel", "arbitrary"), vmem_limit_bytes=VMEM_LIMIT),
        name="attn_decode",
    )(pt_flat, lam, qbd, kn, vn, norm_g, *([cache_k] * G), *([cache_v] * G))


def _oproj_router_body(x_ref, go_ref, do_ref, wo_ref, ln_ref, wr_ref, br_ref,
                       h_ref, hn_ref, gate_ref, idx_ref):
    h = (x_ref[...]
         + jnp.dot(go_ref[...].astype(BF16), wo_ref[0:GROUP_W, :], preferred_element_type=F32)
         + jnp.dot(do_ref[...].astype(BF16), wo_ref[GROUP_W:, :], preferred_element_type=F32))
    h_ref[...] = h
    hn = h * lax.rsqrt(jnp.mean(h * h, axis=-1, keepdims=True) + EPS) * ln_ref[...]
    hn_ref[...] = hn.astype(BF16)
    logits = _dot3(hn, wr_ref[...]) + br_ref[...]
    lane = lax.broadcasted_iota(I32, logits.shape, 1)
    lane_f = lane.astype(F32)
    cur = jnp.where(lane < N_EXPERTS, logits, -jnp.inf)
    vals, idxs = [], []
    for _ in range(TOP_K):
        m = jnp.max(cur, axis=1, keepdims=True)
        i = jnp.min(jnp.where(cur == m, lane_f, float(LANES)), axis=1, keepdims=True)
        vals.append(m)
        idxs.append(i)
        cur = jnp.where(lane_f == i, -jnp.inf, cur)
    es = [jnp.exp(v - vals[0]) for v in vals]
    den = es[0]
    for e in es[1:]:
        den = den + e
    gate = jnp.zeros(logits.shape, F32)
    idx = jnp.zeros(logits.shape, F32)
    for k in range(TOP_K):
        gate = jnp.where(lane == k, es[k] / den, gate)
        idx = jnp.where(lane == k, idxs[k], idx)
    gate_ref[...] = gate
    idx_ref[...] = idx.astype(I32)


def _oproj_router(x, g_out, d_out, w_o, ln_g, w_r, b_r):
    n = x.shape[0]
    tm = _pick_tile(n, 512)
    row = lambda w: pl.BlockSpec((tm, w), lambda i: (i, 0))
    full = lambda a: pl.BlockSpec(a.shape, lambda i: (0,) * a.ndim)
    outs = [(D_MODEL, F32), (D_MODEL, BF16), (LANES, F32), (LANES, I32)]
    return pl.pallas_call(
        _oproj_router_body,
        grid=(n // tm,),
        in_specs=[row(D_MODEL), row(GROUP_W), row(GROUP_W), full(w_o), full(ln_g),
                  full(w_r), full(b_r)],
        out_specs=[row(w) for w, _ in outs],
        out_shape=[jax.ShapeDtypeStruct((n, w), dt) for w, dt in outs],
        compiler_params=pltpu.CompilerParams(
            dimension_semantics=("parallel",), vmem_limit_bytes=VMEM_LIMIT),
        name="oproj_router",
    )(x, g_out, d_out, w_o, ln_g, w_r, b_r)


MOE_TILE = 256


def _moe_body(te_ref, nused_ref, xs_ref, wg_ref, wu_ref, wd_ref, bg_ref, bu_ref, bd_ref,
              ys_ref, wg_bf, wu_bf, wd_bf):
    i = pl.program_id(0)
    e = te_ref[i]
    prev = te_ref[jnp.maximum(i - 1, 0)]

    @pl.when((i == 0) | (e != prev))
    def _cast():
        wg_bf[...] = wg_ref[0].astype(BF16)
        wu_bf[...] = wu_ref[0].astype(BF16)
        wd_bf[...] = wd_ref[0].astype(BF16)

    @pl.when(i < nused_ref[0])
    def _compute():
        x = xs_ref[...]
        gl = jnp.minimum(jnp.dot(x, wg_bf[...], preferred_element_type=F32) + bg_ref[0],
                         SWIGLU_LIMIT)
        lin = jnp.clip(jnp.dot(x, wu_bf[...], preferred_element_type=F32) + bu_ref[0],
                       -SWIGLU_LIMIT, SWIGLU_LIMIT)
        act = gl * _sigmoid(SWIGLU_ALPHA * gl) * (lin + 1.0)
        ys_ref[...] = jnp.dot(act.astype(BF16), wd_bf[...],
                              preferred_element_type=F32) + bd_ref[0]

    @pl.when(i >= nused_ref[0])
    def _idle():
        ys_ref[...] = jnp.zeros(ys_ref.shape, F32)


def _moe_experts(tile_expert, n_used, xs, w_gate, b_gate, w_up, b_up, w_down, b_down):
    r = xs.shape[0]
    tm = MOE_TILE
    d_ff = w_gate.shape[-1]
    wspec = lambda a: pl.BlockSpec((1,) + a.shape[1:], lambda i, te, nu: (te[i], 0, 0))
    grid_spec = pltpu.PrefetchScalarGridSpec(
        num_scalar_prefetch=2,
        grid=(r // tm,),
        in_specs=[pl.BlockSpec((tm, D_MODEL), lambda i, te, nu: (i, 0)),
                  wspec(w_gate), wspec(w_up), wspec(w_down),
                  wspec(b_gate), wspec(b_up), wspec(b_down)],
        out_specs=pl.BlockSpec((tm, D_MODEL), lambda i, te, nu: (i, 0)),
        scratch_shapes=[pltpu.VMEM((D_MODEL, d_ff), BF16), pltpu.VMEM((D_MODEL, d_ff), BF16),
                        pltpu.VMEM((d_ff, D_MODEL), BF16)],
    )
    return pl.pallas_call(
        _moe_body,
        grid_spec=grid_spec,
        out_shape=jax.ShapeDtypeStruct((r, D_MODEL), F32),
        compiler_params=pltpu.CompilerParams(
            dimension_semantics=("arbitrary",), vmem_limit_bytes=VMEM_LIMIT),
        name="moe_experts",
    )(tile_expert, n_used, xs, w_gate, w_up, w_down, b_gate, b_up, b_down)


def _combine_body(h_ref, yg_ref, gate_ref, fg_ref, y_ref):
    gate = gate_ref[...]
    out = h_ref[...]
    for k in range(TOP_K):
        out = out + gate[:, k:k + 1] * yg_ref[:, k * D_MODEL:(k + 1) * D_MODEL]
    y_ref[...] = out * lax.rsqrt(jnp.mean(out * out, axis=-1, keepdims=True) + EPS) * fg_ref[...]


def _combine_norm(h, yg, gate, final_g):
    n = h.shape[0]
    tm = _pick_tile(n, 512)
    row = lambda w: pl.BlockSpec((tm, w), lambda i: (i, 0))
    return pl.pallas_call(
        _combine_body,
        grid=(n // tm,),
        in_specs=[row(D_MODEL), row(TOP_K * D_MODEL), row(LANES),
                  pl.BlockSpec((1, D_MODEL), lambda i: (0, 0))],
        out_specs=row(D_MODEL),
        out_shape=jax.ShapeDtypeStruct((n, D_MODEL), F32),
        compiler_params=pltpu.CompilerParams(
            dimension_semantics=("parallel",), vmem_limit_bytes=VMEM_LIMIT),
        name="combine_norm",
    )(h, yg, gate, final_g)


def _route(top_i, tile):
    n = top_i.shape[0]
    e_flat = top_i.reshape(-1)
    onehot = (e_flat[:, None] == jnp.arange(N_EXPERTS, dtype=I32)[None, :]).astype(I32)
    csum = jnp.cumsum(onehot, axis=0)
    rank = jnp.sum(onehot * csum, axis=1) - 1
    counts = csum[-1]
    padded = ((counts + tile - 1) // tile) * tile
    ends = jnp.cumsum(padded)
    starts = ends - padded
    pos = starts[e_flat] + rank
    n_rows = n * TOP_K + N_EXPERTS * tile
    n_tiles = n_rows // tile
    tile_start = jnp.arange(n_tiles, dtype=I32) * tile
    tile_expert = jnp.minimum(
        jnp.sum((tile_start[:, None] >= ends[None, :]).astype(I32), axis=1), N_EXPERTS - 1)
    n_used = (ends[-1] // tile).astype(I32).reshape(1)
    return pos.astype(I32), tile_expert.astype(I32), n_used, n_rows


def kernel(x_prompt, x_sample, cache_k, cache_v, state_gdn, state_conv, page_table, ln1_g, w_in, conv_w, a_log, dt_bias, gdn_norm_g, lambda_q1, lambda_k1, lambda_q2, lambda_k2, diff_norm_g, w_o, ln2_g, w_router, b_router, w_gate, b_gate, w_up, b_up, w_down, b_down, final_g):
    B, L, _ = x_prompt.shape
    DB, T, _ = x_sample.shape
    depth = w_in.shape[0]
    assert depth == 1
    l = 0
    n_p, n_s = B * L, DB * T
    lam_init = 0.8 - 0.6 * math.exp(-0.3 * l)
    s1 = jnp.sum(lambda_q1[l].astype(F32) * lambda_k1[l].astype(F32))
    s2 = jnp.sum(lambda_q2[l].astype(F32) * lambda_k2[l].astype(F32))
    lam = (jnp.exp(s1) - jnp.exp(s2) + lam_init).reshape(1, 1).astype(F32)

    x_all = jnp.concatenate([x_prompt.reshape(n_p, D_MODEL), x_sample.reshape(n_s, D_MODEL)], axis=0)
    wl = w_in[l]
    o_z = CONV_DIM
    o_a = o_z + GROUP_W
    o_q = o_a + 2 * N_HEADS
    w_cat = jnp.concatenate(
        [wl[:, :o_a], wl[:, o_q:], wl[:, o_a:o_q],
         jnp.zeros((D_MODEL, LANES - 2 * N_HEADS), wl.dtype)], axis=1).astype(BF16)
    qkv, z, kd, vd, ab, qdb, kdb, vdb = _inproj(x_all, ln1_g[l].reshape(1, D_MODEL), w_cat)

    gpar = jnp.zeros((8, LANES), F32)
    gpar = gpar.at[0, :N_HEADS].set(a_log[l].astype(F32)).at[1, :N_HEADS].set(dt_bias[l].astype(F32))
    norm_g = gdn_norm_g[l].reshape(1, HEAD_W).astype(F32)
    cw = conv_w[l].astype(F32)
    g_out_p, s_p = _gdn(
        qkv[:n_p].reshape(B, L, CONV_DIM), ab[:n_p].reshape(B, L, LANES),
        z[:n_p].reshape(B, L, GROUP_W),
        jnp.zeros((B, HIST_ROWS, CONV_DIM), F32),
        jnp.zeros((B, N_HEADS, HEAD_W, HEAD_W), F32), cw, gpar, norm_g, valid=GDN_CHUNK)

    def pad_tok(a):
        return jnp.pad(a, ((0, 0), (0, Q_ROWS - T), (0, 0)))

    qkv_s = qkv[n_p:].reshape(DB, T, CONV_DIM)
    hist_s = jnp.pad(state_conv[l].astype(F32), ((0, 0), (HIST_ROWS - (CONV_W - 1), 0), (0, 0)))
    g_out_s, s_s = _gdn(
        pad_tok(qkv_s), pad_tok(ab[n_p:].reshape(DB, T, LANES)),
        pad_tok(z[n_p:].reshape(DB, T, GROUP_W)), hist_s, state_gdn[l].astype(F32),
        cw, gpar, norm_g, valid=T)
    g_out_s = g_out_s[:, :T].reshape(n_s, GROUP_W)

    dn_g = diff_norm_g[l].reshape(1, HEAD_W).astype(F32)
    d_out_p = _attn_prompt(lam, qdb, kdb, vdb, dn_g, batch=B, seq=L, out_scale=1.0 - lam_init)

    q_s = qdb[n_p:].reshape(DB, T, GROUP_W)
    col = jnp.arange(GROUP_W)
    sel = ((col[None, None, :] // HEAD_W == jnp.arange(N_HEADS)[None, :, None])
           & ((col[None, None, :] % HEAD_W) // DIFF_HALF == jnp.arange(2)[:, None, None]))
    qbd = jnp.where(sel[None, :, :, None, :], pad_tok(q_s)[:, None, None, :, :],
                    jnp.zeros((), BF16)).reshape(DB, QBD_ROWS, GROUP_W)
    d_out_s = _attn_decode(
        page_table, lam, qbd, pad_tok(kd[n_p:].reshape(DB, T, GROUP_W)),
        pad_tok(vd[n_p:].reshape(DB, T, GROUP_W)), dn_g,
        cache_k[l].reshape(-1, PAGE, GROUP_W), cache_v[l].reshape(-1, PAGE, GROUP_W),
        n_new=T, out_scale=1.0 - lam_init)
    d_out_s = d_out_s[:, :T].reshape(n_s, GROUP_W)

    g_out = jnp.concatenate([g_out_p.reshape(n_p, GROUP_W), g_out_s], axis=0)
    d_out = jnp.concatenate([d_out_p, d_out_s], axis=0)
    w_r = jnp.pad(w_router[l].astype(F32), ((0, 0), (0, LANES - N_EXPERTS)))
    b_r = jnp.pad(b_router[l].astype(F32), (0, LANES - N_EXPERTS)).reshape(1, LANES)
    h, hn, gate, idx = _oproj_router(x_all, g_out, d_out, w_o[l].astype(BF16),
                                     ln2_g[l].reshape(1, D_MODEL), w_r, b_r)

    pos, tile_expert, n_used, n_rows = _route(idx[:, :TOP_K], MOE_TILE)
    xs = jnp.zeros((n_rows, D_MODEL), BF16).at[pos].set(jnp.repeat(hn, TOP_K, axis=0))
    ys = _moe_experts(tile_expert, n_used, xs, w_gate[l], b_gate[l][:, None, :],
                      w_up[l], b_up[l][:, None, :], w_down[l], b_down[l][:, None, :])

    yg = jnp.take(ys, pos, axis=0).reshape(n_p + n_s, TOP_K * D_MODEL)
    y = _combine_norm(h, yg, gate, final_g.reshape(1, D_MODEL).astype(F32))

    dt_k, dt_v = cache_k.dtype, cache_v.dtype
    return (y[:n_p].reshape(B, L, D_MODEL),
            y[n_p:].reshape(DB, T, D_MODEL),
            kd[:n_p].reshape(1, B, L, N_HEADS, HEAD_W).astype(dt_k),
            vd[:n_p].reshape(1, B, L, N_HEADS, HEAD_W).astype(dt_v),
            kd[n_p:].reshape(1, DB, T, N_HEADS, HEAD_W).astype(dt_k),
            vd[n_p:].reshape(1, DB, T, N_HEADS, HEAD_W).astype(dt_v),
            s_p[None].astype(state_gdn.dtype),
            qkv[:n_p].reshape(B, L, CONV_DIM)[None, :, L - (CONV_W - 1):].astype(state_conv.dtype),
            s_s[None].astype(state_gdn.dtype),
            qkv_s[None, :, T - (CONV_W - 1):].astype(state_conv.dtype))
```

```python
import functools
import math

import jax
import jax.numpy as jnp
from jax import lax
from jax.experimental import pallas as pl
from jax.experimental.pallas import tpu as pltpu

F32 = jnp.float32
BF16 = jnp.bfloat16
I32 = jnp.int32

D_MODEL = 1024
HEAD_W = 128
N_HEADS = 4
GROUP_W = N_HEADS * HEAD_W
CONV_W = 4
CONV_DIM = 3 * GROUP_W
GDN_CHUNK = 64
DIFF_HALF = HEAD_W // 2
PAGE = 128
N_EXPERTS = 32
TOP_K = 4
SWIGLU_LIMIT = 7.0
SWIGLU_ALPHA = 1.702
EPS = 1e-6
LANES = 128
SUBLANES = 8
LOG2E = 1.4426950408889634
HIST_ROWS = 8
D_IN_PAD = CONV_DIM + 4 * GROUP_W + LANES
VMEM_LIMIT = 56 * 1024 * 1024


def _pick_tile(n, pref, mult=16):
    t = min(pref, n)
    while t > mult and (n % t or t % mult):
        t -= mult
    assert n % t == 0, (n, pref)
    return t


def _dot(a, b):
    return jnp.dot(a.astype(BF16), b.astype(BF16), preferred_element_type=F32)


def _dot_nt(a, b):
    return lax.dot_general(a.astype(BF16), b.astype(BF16), (((1,), (1,)), ((), ())),
                           preferred_element_type=F32)


def _dot_tn(a, b):
    return lax.dot_general(a.astype(BF16), b.astype(BF16), (((0,), (0,)), ((), ())),
                           preferred_element_type=F32)


def _split2(x):
    hi = x.astype(BF16)
    lo = (x - hi.astype(F32)).astype(BF16)
    return hi, lo


def _dot3(a, b, dot=_dot):
    ah, al = _split2(a)
    bh, bl = _split2(b)
    return dot(ah, bh) + (dot(ah, bl) + dot(al, bh))


def _sigmoid(x):
    return 1.0 / (1.0 + jnp.exp(-x))


TOKEN_TILE = 512


def _prompt_spec(w, npt):
    return pl.BlockSpec((TOKEN_TILE, w), lambda i: (jnp.minimum(i, npt - 1), 0))


def _sample_spec(w, npt):
    return pl.BlockSpec((TOKEN_TILE, w), lambda i: (jnp.maximum(i - npt, 0), 0))


def _inproj_body(xp_ref, xs_ref, g_ref, w_ref, qkv_ref, z_ref, ab_ref, qdb_ref, kdb_ref,
                 kdp_ref, kds_ref, vdp_ref, vds_ref, qt_ref, vt_ref, *, npt):
    i = pl.program_id(0)
    x = jnp.where(i < npt, xp_ref[...], xs_ref[...])
    ms = jnp.mean(x * x, axis=-1, keepdims=True)
    xn = (x * lax.rsqrt(ms + EPS) * g_ref[...]).astype(BF16)

    def seg(lo, hi):
        return jnp.dot(xn, w_ref[:, lo:hi], preferred_element_type=F32)

    o = CONV_DIM
    qkv_ref[...] = seg(0, o)
    z_ref[...] = seg(o, o + GROUP_W)
    ab_ref[...] = seg(o + 4 * GROUP_W, o + 4 * GROUP_W + LANES)
    qd = seg(o + GROUP_W, o + 2 * GROUP_W) * (DIFF_HALF ** -0.5 * LOG2E)
    qdb_ref[...] = qd.astype(BF16)
    kd = seg(o + 2 * GROUP_W, o + 3 * GROUP_W)
    kdb_ref[...] = kd.astype(BF16)
    vd = seg(o + 3 * GROUP_W, o + 4 * GROUP_W)

    @pl.when(i < npt)
    def _prompt():
        kdp_ref[...] = kd
        vdp_ref[...] = vd
        qt_ref[...] = qd.T.astype(BF16)
        vt_ref[...] = vd.T.astype(BF16).reshape(N_HEADS, HEAD_W, TOKEN_TILE)

    @pl.when(i >= npt)
    def _sample():
        kds_ref[...] = kd
        vds_ref[...] = vd


def _inproj(x_p, x_s, ln_g, w_cat, *, batch):
    n_p, n_s = x_p.shape[0], x_s.shape[0]
    tm = TOKEN_TILE
    assert n_p % (batch * tm) == 0 and n_s % tm == 0
    npt, nst = n_p // tm, n_s // tm
    nkb = npt // batch
    n = n_p + n_s
    row = lambda w: pl.BlockSpec((tm, w), lambda i: (i, 0))
    full = lambda a: pl.BlockSpec(a.shape, lambda i: (0,) * a.ndim)

    def vt_map(i):
        t = jnp.minimum(i, npt - 1)
        return (t // nkb, 0, t % nkb, 0, 0)

    sds = jax.ShapeDtypeStruct
    return pl.pallas_call(
        functools.partial(_inproj_body, npt=npt),
        grid=(npt + nst,),
        in_specs=[_prompt_spec(D_MODEL, npt), _sample_spec(D_MODEL, npt), full(ln_g), full(w_cat)],
        out_specs=[row(CONV_DIM), row(GROUP_W), row(LANES), row(GROUP_W), row(GROUP_W),
                   _prompt_spec(GROUP_W, npt), _sample_spec(GROUP_W, npt),
                   _prompt_spec(GROUP_W, npt), _sample_spec(GROUP_W, npt),
                   pl.BlockSpec((GROUP_W, tm), lambda i: (0, jnp.minimum(i, npt - 1))),
                   pl.BlockSpec((None, N_HEADS, None, HEAD_W, tm), vt_map)],
        out_shape=[sds((n, CONV_DIM), F32), sds((n, GROUP_W), F32), sds((n, LANES), F32),
                   sds((n, GROUP_W), BF16), sds((n, GROUP_W), BF16),
                   sds((n_p, GROUP_W), F32), sds((n_s, GROUP_W), F32),
                   sds((n_p, GROUP_W), F32), sds((n_s, GROUP_W), F32),
                   sds((GROUP_W, n_p), BF16),
                   sds((batch, N_HEADS, nkb, HEAD_W, tm), BF16)],
        compiler_params=pltpu.CompilerParams(
            dimension_semantics=("arbitrary",), vmem_limit_bytes=VMEM_LIMIT),
        name="inproj",
    )(x_p, x_s, ln_g, w_cat)


def _unit_lower_inverse(a, c):
    r = lax.broadcasted_iota(I32, (c, c), 0)
    q = lax.broadcasted_iota(I32, (c, c), 1)
    t = jnp.where(r == q, 1.0, 0.0).astype(F32) - a
    p = a
    n = 1
    while 2 * n < c:
        p = _dot3(p, p)
        t = t + _dot3(t, p)
        n *= 2
    return t


def _gdn_body(qkv_ref, ab_ref, z_ref, hist0_ref, s0_ref, convw_ref, gpar_ref, ng_ref,
              out_ref, sfin_ref, s_scr, hist_scr, *, chunk, cin, valid):
    c = pl.program_id(1)
    C = chunk

    @pl.when(c == 0)
    def _init():
        s_scr[...] = s0_ref[0]
        hist_scr[...] = hist0_ref[0]

    def pad_rows(t):
        if cin == C:
            return t
        return jnp.concatenate([t, jnp.zeros((C - cin, t.shape[1]), t.dtype)], axis=0)

    x = pad_rows(qkv_ref[...])
    xp = jnp.concatenate([hist_scr[...], x], axis=0)
    w = convw_ref[...]
    h0 = HIST_ROWS - (CONV_W - 1)
    conv = xp[h0:h0 + C] * w[0:1]
    for j in range(1, CONV_W):
        conv = conv + xp[h0 + j:h0 + j + C] * w[j:j + 1]
    hist_scr[...] = x[C - HIST_ROWS:C]
    act = conv * _sigmoid(conv)

    t = pad_rows(ab_ref[...])
    sp_in = t + gpar_ref[1:2, :]
    softplus = jnp.maximum(sp_in, 0.0) + jnp.log(1.0 + jnp.exp(-jnp.abs(sp_in)))
    gval = -jnp.exp(gpar_ref[0:1, :]) * softplus
    bval = _sigmoid(t)
    lane = lax.broadcasted_iota(I32, (C, LANES), 1)
    rowi = lax.broadcasted_iota(I32, (C, LANES), 0)
    gb = jnp.where(lane < N_HEADS, gval, bval)
    if valid < C:
        gb = jnp.where(rowi < valid, gb, 0.0)

    r = lax.broadcasted_iota(I32, (C, C), 0)
    q_ = lax.broadcasted_iota(I32, (C, C), 1)
    incl = r >= q_
    strict = r > q_
    tril = jnp.where(incl, 1.0, 0.0).astype(BF16)
    g_hi = gb.astype(BF16)
    g_r1 = gb - g_hi.astype(F32)
    g_mid = g_r1.astype(BF16)
    g_lo = (g_r1 - g_mid.astype(F32)).astype(BF16)
    gam = (jnp.dot(tril, g_hi, preferred_element_type=F32)
           + jnp.dot(tril, g_mid, preferred_element_type=F32)
           + jnp.dot(tril, g_lo, preferred_element_type=F32))
    gam_t = gam.T

    z = z_ref[...]
    for h in range(N_HEADS):
        q = act[:, h * HEAD_W:(h + 1) * HEAD_W]
        k = act[:, GROUP_W + h * HEAD_W:GROUP_W + (h + 1) * HEAD_W]
        v = act[:, 2 * GROUP_W + h * HEAD_W:2 * GROUP_W + (h + 1) * HEAD_W]
        q = q * lax.rsqrt(jnp.sum(q * q, axis=-1, keepdims=True) + EPS) * (HEAD_W ** -0.5)
        k = k * lax.rsqrt(jnp.sum(k * k, axis=-1, keepdims=True) + EPS)
        gcol = jnp.broadcast_to(gam[:, h:h + 1], (C, HEAD_W))
        bcol = jnp.broadcast_to(gb[:, N_HEADS + h:N_HEADS + h + 1], (C, HEAD_W))
        grow = jnp.broadcast_to(gam_t[h:h + 1, :], (C, C))
        dec = jnp.where(incl, jnp.exp(jnp.minimum(gcol[:, :C] - grow, 0.0)), 0.0)
        eg = jnp.exp(gcol)
        kk = _dot_nt(k, k)
        a_mat = jnp.where(strict, bcol[:, :C] * kk * dec, 0.0)
        t_inv = _unit_lower_inverse(a_mat, C)
        rhs = jnp.concatenate([bcol * v, bcol * k * eg], axis=1)
        sol = _dot3(t_inv, rhs)
        u, w_ = sol[:, :HEAD_W], sol[:, HEAD_W:]
        s_old = s_scr[h]
        wq_s = _dot3(jnp.concatenate([w_, q], axis=0), s_old)
        delta = u - wq_s[:C]
        qk = _dot_nt(q, k) * dec
        o = eg * wq_s[C:] + _dot3(qk, delta)
        glast = gam[C - 1:C, h:h + 1]
        tail = jnp.exp(jnp.broadcast_to(glast, (C, HEAD_W)) - gcol)
        s_new = (jnp.broadcast_to(jnp.exp(glast), (HEAD_W, HEAD_W)) * s_old
                 + _dot3(k * tail, delta, dot=_dot_tn))
        s_scr[h] = s_new
        on = o * lax.rsqrt(jnp.mean(o * o, axis=-1, keepdims=True) + EPS) * ng_ref[...]
        zh = z[:, h * HEAD_W:(h + 1) * HEAD_W]
        out_ref[:, h * HEAD_W:(h + 1) * HEAD_W] = on[:cin] * (zh * _sigmoid(zh))

    @pl.when(c == pl.num_programs(1) - 1)
    def _fin():
        sfin_ref[0] = s_scr[...]


def _gdn(qkv, ab, z, hist0, s0, conv_w, gpar, norm_g, *, b, l, valid):
    cin = min(l, GDN_CHUNK)
    assert l % cin == 0 and (l == cin or valid == cin)
    nc = l // cin
    tok = lambda w: pl.BlockSpec((cin, w), lambda i, c: (i * nc + c, 0))
    per_b = lambda a: pl.BlockSpec((1,) + a.shape[1:], lambda i, c: (i,) + (0,) * (a.ndim - 1))
    full = lambda a: pl.BlockSpec(a.shape, lambda i, c: (0,) * a.ndim)
    return pl.pallas_call(
        functools.partial(_gdn_body, chunk=GDN_CHUNK, cin=cin, valid=valid),
        grid=(b, nc),
        in_specs=[tok(CONV_DIM), tok(LANES), tok(GROUP_W), per_b(hist0), per_b(s0),
                  full(conv_w), full(gpar), full(norm_g)],
        out_specs=[tok(GROUP_W), per_b(s0)],
        out_shape=[jax.ShapeDtypeStruct((b * l, GROUP_W), F32),
                   jax.ShapeDtypeStruct(s0.shape, F32)],
        scratch_shapes=[pltpu.VMEM((N_HEADS, HEAD_W, HEAD_W), F32),
                        pltpu.VMEM((HIST_ROWS, CONV_DIM), F32)],
        compiler_params=pltpu.CompilerParams(
            dimension_semantics=("parallel", "arbitrary"), vmem_limit_bytes=VMEM_LIMIT),
        name="gdn",
    )(qkv, ab, z, hist0, s0, conv_w, gpar, norm_g)


def _rep_lanes(x, n):
    return x if n == 1 else jnp.concatenate([x] * n, axis=1)


def _sublane_all(x, op):
    for shift in (4, 2, 1):
        x = op(x, pltpu.roll(x, shift, axis=0))
    return x


def _attn_prompt_body(lam_ref, qt_ref, k_ref, vt_ref, ng_ref, o_ref, m_scr, l_scr, acc_scr,
                      *, tq, tk, out_scale):
    qi = pl.program_id(2)
    qt = qt_ref[...]
    row = lax.broadcasted_iota(I32, qt.shape, 0)
    zero = jnp.zeros_like(qt)
    qqt = jnp.concatenate([jnp.where(row < DIFF_HALF, qt, zero),
                           jnp.where(row >= DIFF_HALF, qt, zero)], axis=1)
    m_scr[...] = jnp.full(m_scr.shape, -jnp.inf, F32)
    l_scr[...] = jnp.zeros(l_scr.shape, F32)
    acc_scr[...] = jnp.zeros(acc_scr.shape, F32)
    nq2 = 2 * tq

    def block(kb, masked):
        start = pl.multiple_of(kb * tk, tk)
        k = k_ref[pl.ds(start, tk), :]
        s = jnp.dot(k, qqt, preferred_element_type=F32)
        if masked:
            key = lax.broadcasted_iota(I32, s.shape, 0) + (kb * tk - qi * tq)
            qc = lax.broadcasted_iota(I32, s.shape, 1)
            qc = jnp.where(qc >= tq, qc - tq, qc)
            s = jnp.where(key <= qc, s, -jnp.inf)
        s3 = s.reshape(tk // SUBLANES, SUBLANES, nq2)
        m_prev = m_scr[...]
        m_new = jnp.maximum(m_prev, _sublane_all(jnp.max(s3, axis=0), jnp.maximum))
        alpha = jnp.exp2(m_prev - m_new)
        p3 = jnp.exp2(s3 - m_new[None])
        l_scr[...] = alpha * l_scr[...] + _sublane_all(jnp.sum(p3, axis=0), jnp.add)
        pv = jnp.dot(vt_ref[kb], p3.reshape(tk, nq2).astype(BF16),
                     preferred_element_type=F32)
        acc = acc_scr[...].reshape(HEAD_W // SUBLANES, SUBLANES, nq2) * alpha[None]
        acc_scr[...] = acc.reshape(HEAD_W, nq2) + pv
        m_scr[...] = m_new

    def body(kb, carry):
        block(kb, False)
        return carry

    def body_masked(kb, carry):
        block(kb, True)
        return carry

    n_full = (qi * tq) // tk
    n_all = (qi * tq + tq + tk - 1) // tk
    lax.fori_loop(0, n_full, body, 0)
    lax.fori_loop(n_full, n_all, body_masked, 0)

    ot = acc_scr[...] / l_scr[0:1, :]
    ot = ot[:, :tq] - lam_ref[0, 0] * ot[:, tq:]
    o = ot.T
    o = o * lax.rsqrt(jnp.mean(o * o, axis=-1, keepdims=True) + EPS) * ng_ref[...]
    o_ref[...] = o * out_scale


def _attn_prompt(lam, qt, kdb, vt, norm_g, *, batch, seq, tk, out_scale):
    tq = _pick_tile(seq, 512, LANES)
    nq = seq // tq
    return pl.pallas_call(
        functools.partial(_attn_prompt_body, tq=tq, tk=tk, out_scale=out_scale),
        grid=(batch, N_HEADS, nq),
        in_specs=[pl.BlockSpec(memory_space=pltpu.SMEM),
                  pl.BlockSpec((HEAD_W, tq), lambda b, h, i: (h, b * nq + i)),
                  pl.BlockSpec((seq, HEAD_W), lambda b, h, i: (b, h)),
                  pl.BlockSpec((None, None, seq // tk, HEAD_W, tk),
                               lambda b, h, i: (b, h, 0, 0, 0)),
                  pl.BlockSpec((1, HEAD_W), lambda b, h, i: (0, 0))],
        out_specs=pl.BlockSpec((tq, HEAD_W), lambda b, h, i: (b * nq + i, h)),
        out_shape=jax.ShapeDtypeStruct((batch * seq, GROUP_W), F32),
        scratch_shapes=[pltpu.VMEM((SUBLANES, 2 * tq), F32), pltpu.VMEM((SUBLANES, 2 * tq), F32),
                        pltpu.VMEM((HEAD_W, 2 * tq), F32)],
        compiler_params=pltpu.CompilerParams(
            dimension_semantics=("parallel", "parallel", "arbitrary"),
            vmem_limit_bytes=VMEM_LIMIT),
        name="attn_prompt",
    )(lam, qt, kdb, vt, norm_g)


Q_ROWS = 8
QBD_ROWS = 2 * N_HEADS * Q_ROWS


def _attn_decode_body(pt_ref, lam_ref, qbd_ref, kn_ref, vn_ref, ng_ref, *rest,
                      pages_per_step, n_new, out_scale):
    G = pages_per_step
    k_refs, v_refs = rest[:G], rest[G:2 * G]
    o_ref, m_scr, l_scr, acc_scr = rest[2 * G:]
    j = pl.program_id(1)

    @pl.when(j == 0)
    def _init():
        m_scr[...] = jnp.full(m_scr.shape, -jnp.inf, F32)
        l_scr[...] = jnp.zeros(l_scr.shape, F32)
        acc_scr[...] = jnp.zeros(acc_scr.shape, F32)

    qbd = qbd_ref[0]

    def update(s, vs):
        n = len(vs)
        m_prev = m_scr[...]
        m_new = jnp.maximum(m_prev, jnp.max(s, axis=1, keepdims=True))
        alpha = jnp.exp2(m_prev - m_new)
        p = jnp.exp2(s - _rep_lanes(m_new, n)).astype(BF16)
        l_scr[...] = alpha * l_scr[...] + jnp.sum(p.astype(F32), axis=1, keepdims=True)
        pv = jnp.dot(p[:, :LANES], vs[0], preferred_element_type=F32)
        for g in range(1, n):
            pv = pv + jnp.dot(p[:, g * LANES:(g + 1) * LANES], vs[g],
                              preferred_element_type=F32)
        acc_scr[...] = _rep_lanes(alpha, GROUP_W // LANES) * acc_scr[...] + pv
        m_scr[...] = m_new

    s = jnp.concatenate([_dot_nt(qbd, k_refs[g][...]) for g in range(G)], axis=1)
    update(s, [v_refs[g][...].astype(BF16) for g in range(G)])

    @pl.when(j == pl.num_programs(1) - 1)
    def _fin():
        pad = jnp.zeros((PAGE - Q_ROWS, GROUP_W), F32)
        kn = jnp.concatenate([kn_ref[0], pad], axis=0)
        vn = jnp.concatenate([vn_ref[0], pad], axis=0).astype(BF16)
        sn = _dot_nt(qbd, kn)
        t = lax.broadcasted_iota(I32, sn.shape, 0) & (Q_ROWS - 1)
        cidx = lax.broadcasted_iota(I32, sn.shape, 1)
        sn = jnp.where((cidx <= t) & (cidx < n_new), sn, -jnp.inf)
        update(sn, [vn])
        acc = acc_scr[...]
        l = l_scr[...]
        lam = lam_ref[0, 0]
        half = N_HEADS * Q_ROWS
        for h in range(N_HEADS):
            rows = slice(h * Q_ROWS, (h + 1) * Q_ROWS)
            rows2 = slice(half + h * Q_ROWS, half + (h + 1) * Q_ROWS)
            cols = slice(h * HEAD_W, (h + 1) * HEAD_W)
            o = acc[rows, cols] / l[rows] - lam * (acc[rows2, cols] / l[rows2])
            o = o * lax.rsqrt(jnp.mean(o * o, axis=-1, keepdims=True) + EPS) * ng_ref[...]
            o_ref[0, :, cols] = o * out_scale


def _attn_decode(page_table, lam, qbd, kn, vn, norm_g, cache_k, cache_v, *, n_new, out_scale):
    db, n_pages = page_table.shape
    G = 8
    while n_pages % G:
        G //= 2
    nj = n_pages // G
    pt_flat = page_table.reshape(-1)

    def page_map(g):
        return lambda b, j, pt: (pt[b * n_pages + j * G + g], 0, 0)

    per_b = lambda a: pl.BlockSpec((1,) + a.shape[1:], lambda b, j, pt: (b, 0, 0))
    page_spec = lambda g: pl.BlockSpec((None, PAGE, GROUP_W), page_map(g))
    grid_spec = pltpu.PrefetchScalarGridSpec(
        num_scalar_prefetch=1,
        grid=(db, nj),
        in_specs=[pl.BlockSpec(memory_space=pltpu.SMEM), per_b(qbd), per_b(kn), per_b(vn),
                  pl.BlockSpec((1, HEAD_W), lambda b, j, pt: (0, 0))]
                 + [page_spec(g) for g in range(G)] + [page_spec(g) for g in range(G)],
        out_specs=pl.BlockSpec((1, Q_ROWS, GROUP_W), lambda b, j, pt: (b, 0, 0)),
        scratch_shapes=[pltpu.VMEM((QBD_ROWS, LANES), F32), pltpu.VMEM((QBD_ROWS, LANES), F32),
                        pltpu.VMEM((QBD_ROWS, GROUP_W), F32)],
    )
    return pl.pallas_call(
        functools.partial(_attn_decode_body, pages_per_step=G, n_new=n_new,
                          out_scale=out_scale),
        grid_spec=grid_spec,
        out_shape=jax.ShapeDtypeStruct((db, Q_ROWS, GROUP_W), F32),
        compiler_params=pltpu.CompilerParams(
            dimension_semantics=("parallel", "arbitrary"), vmem_limit_bytes=VMEM_LIMIT),
        name="attn_decode",
    )(pt_flat, lam, qbd, kn, vn, norm_g, *([cache_k] * G), *([cache_v] * G))


def _oproj_router_body(xp_ref, xs_ref, gop_ref, gos_ref, dop_ref, dos_ref, wo_ref, ln_ref,
                       wr_ref, br_ref, h_ref, hn_ref, gate_ref, idx_ref, *, npt):
    is_p = pl.program_id(0) < npt
    x = jnp.where(is_p, xp_ref[...], xs_ref[...])
    go = jnp.where(is_p, gop_ref[...], gos_ref[...])
    do = jnp.where(is_p, dop_ref[...], dos_ref[...])
    h = (x
         + jnp.dot(go.astype(BF16), wo_ref[0:GROUP_W, :], preferred_element_type=F32)
         + jnp.dot(do.astype(BF16), wo_ref[GROUP_W:, :], preferred_element_type=F32))
    h_ref[...] = h
    hn = h * lax.rsqrt(jnp.mean(h * h, axis=-1, keepdims=True) + EPS) * ln_ref[...]
    hn_ref[...] = hn.astype(BF16)
    logits = _dot3(hn, wr_ref[...]) + br_ref[...]
    lane = lax.broadcasted_iota(I32, logits.shape, 1)
    lane_f = lane.astype(F32)
    cur = jnp.where(lane < N_EXPERTS, logits, -jnp.inf)
    vals, idxs = [], []
    for _ in range(TOP_K):
        m = jnp.max(cur, axis=1, keepdims=True)
        i = jnp.min(jnp.where(cur == m, lane_f, float(LANES)), axis=1, keepdims=True)
        vals.append(m)
        idxs.append(i)
        cur = jnp.where(lane_f == i, -jnp.inf, cur)
    es = [jnp.exp(v - vals[0]) for v in vals]
    den = es[0]
    for e in es[1:]:
        den = den + e
    gate = jnp.zeros(logits.shape, F32)
    idx = jnp.zeros(logits.shape, F32)
    for k in range(TOP_K):
        gate = jnp.where(lane == k, es[k] / den, gate)
        idx = jnp.where(lane == k, idxs[k], idx)
    gate_ref[...] = gate
    idx_ref[...] = idx.astype(I32)


def _oproj_router(x_p, x_s, go_p, go_s, do_p, do_s, w_o, ln_g, w_r, b_r):
    n_p, n_s = x_p.shape[0], x_s.shape[0]
    tm = TOKEN_TILE
    npt, nst = n_p // tm, n_s // tm
    n = n_p + n_s
    row = lambda w: pl.BlockSpec((tm, w), lambda i: (i, 0))
    full = lambda a: pl.BlockSpec(a.shape, lambda i: (0,) * a.ndim)
    outs = [(D_MODEL, F32), (D_MODEL, BF16), (LANES, F32), (LANES, I32)]
    return pl.pallas_call(
        functools.partial(_oproj_router_body, npt=npt),
        grid=(npt + nst,),
        in_specs=[_prompt_spec(D_MODEL, npt), _sample_spec(D_MODEL, npt),
                  _prompt_spec(GROUP_W, npt), _sample_spec(GROUP_W, npt),
                  _prompt_spec(GROUP_W, npt), _sample_spec(GROUP_W, npt),
                  full(w_o), full(ln_g), full(w_r), full(b_r)],
        out_specs=[row(w) for w, _ in outs],
        out_shape=[jax.ShapeDtypeStruct((n, w), dt) for w, dt in outs],
        compiler_params=pltpu.CompilerParams(
            dimension_semantics=("parallel",), vmem_limit_bytes=VMEM_LIMIT),
        name="oproj_router",
    )(x_p, x_s, go_p, go_s, do_p, do_s, w_o, ln_g, w_r, b_r)


MOE_TILE = 256


def _moe_body(te_ref, nused_ref, xs_ref, wg_ref, wu_ref, wd_ref, bg_ref, bu_ref, bd_ref,
              ys_ref, wg_bf, wu_bf, wd_bf):
    i = pl.program_id(0)
    e = te_ref[i]
    prev = te_ref[jnp.maximum(i - 1, 0)]

    @pl.when((i == 0) | (e != prev))
    def _cast():
        wg_bf[...] = wg_ref[0].astype(BF16)
        wu_bf[...] = wu_ref[0].astype(BF16)
        wd_bf[...] = wd_ref[0].astype(BF16)

    @pl.when(i < nused_ref[0])
    def _compute():
        x = xs_ref[...]
        gl = jnp.minimum(jnp.dot(x, wg_bf[...], preferred_element_type=F32) + bg_ref[0],
                         SWIGLU_LIMIT)
        lin = jnp.clip(jnp.dot(x, wu_bf[...], preferred_element_type=F32) + bu_ref[0],
                       -SWIGLU_LIMIT, SWIGLU_LIMIT)
        act = gl * _sigmoid(SWIGLU_ALPHA * gl) * (lin + 1.0)
        ys_ref[...] = jnp.dot(act.astype(BF16), wd_bf[...],
                              preferred_element_type=F32) + bd_ref[0]

    @pl.when(i >= nused_ref[0])
    def _idle():
        ys_ref[...] = jnp.zeros(ys_ref.shape, F32)


def _moe_experts(tile_expert, n_used, xs, w_gate, b_gate, w_up, b_up, w_down, b_down):
    r = xs.shape[0]
    tm = MOE_TILE
    d_ff = w_gate.shape[-1]
    wspec = lambda a: pl.BlockSpec((1,) + a.shape[1:], lambda i, te, nu: (te[i], 0, 0))
    grid_spec = pltpu.PrefetchScalarGridSpec(
        num_scalar_prefetch=2,
        grid=(r // tm,),
        in_specs=[pl.BlockSpec((tm, D_MODEL), lambda i, te, nu: (i, 0)),
                  wspec(w_gate), wspec(w_up), wspec(w_down),
                  wspec(b_gate), wspec(b_up), wspec(b_down)],
        out_specs=pl.BlockSpec((tm, D_MODEL), lambda i, te, nu: (i, 0)),
        scratch_shapes=[pltpu.VMEM((D_MODEL, d_ff), BF16), pltpu.VMEM((D_MODEL, d_ff), BF16),
                        pltpu.VMEM((d_ff, D_MODEL), BF16)],
    )
    return pl.pallas_call(
        _moe_body,
        grid_spec=grid_spec,
        out_shape=jax.ShapeDtypeStruct((r, D_MODEL), F32),
        compiler_params=pltpu.CompilerParams(
            dimension_semantics=("arbitrary",), vmem_limit_bytes=VMEM_LIMIT),
        name="moe_experts",
    )(tile_expert, n_used, xs, w_gate, w_up, w_down, b_gate, b_up, b_down)


def _combine_body(h_ref, yg_ref, gate_ref, fg_ref, yp_ref, ys_ref, *, npt):
    gate = gate_ref[...]
    out = h_ref[...]
    for k in range(TOP_K):
        out = out + gate[:, k:k + 1] * yg_ref[k]
    y = out * lax.rsqrt(jnp.mean(out * out, axis=-1, keepdims=True) + EPS) * fg_ref[...]
    i = pl.program_id(0)

    @pl.when(i < npt)
    def _prompt():
        yp_ref[...] = y

    @pl.when(i >= npt)
    def _sample():
        ys_ref[...] = y


def _combine_norm(h, yg, gate, final_g, *, n_p):
    n = h.shape[0]
    tm = TOKEN_TILE
    npt = n_p // tm
    row = lambda w: pl.BlockSpec((tm, w), lambda i: (i, 0))
    return pl.pallas_call(
        functools.partial(_combine_body, npt=npt),
        grid=(n // tm,),
        in_specs=[row(D_MODEL), pl.BlockSpec((TOP_K, tm, D_MODEL), lambda i: (0, i, 0)),
                  row(LANES), pl.BlockSpec((1, D_MODEL), lambda i: (0, 0))],
        out_specs=[_prompt_spec(D_MODEL, npt), _sample_spec(D_MODEL, npt)],
        out_shape=[jax.ShapeDtypeStruct((n_p, D_MODEL), F32),
                   jax.ShapeDtypeStruct((n - n_p, D_MODEL), F32)],
        compiler_params=pltpu.CompilerParams(
            dimension_semantics=("arbitrary",), vmem_limit_bytes=VMEM_LIMIT),
        name="combine_norm",
    )(h, yg, gate, final_g)


def _route(top_i, tile):
    n = top_i.shape[0]
    n_assign = n * TOP_K
    e_flat = top_i.reshape(-1)
    onehot = (e_flat[:, None] == jnp.arange(N_EXPERTS, dtype=I32)[None, :]).astype(I32)
    csum = jnp.cumsum(onehot, axis=0)
    rank = jnp.sum(onehot * csum, axis=1) - 1
    counts = csum[-1]
    padded = ((counts + tile - 1) // tile) * tile
    ends = jnp.cumsum(padded)
    starts = ends - padded
    pos = starts[e_flat] + rank
    n_rows = n_assign + N_EXPERTS * tile
    n_tiles = n_rows // tile
    tile_start = jnp.arange(n_tiles, dtype=I32) * tile
    tile_expert = jnp.minimum(
        jnp.sum((tile_start[:, None] >= ends[None, :]).astype(I32), axis=1), N_EXPERTS - 1)
    n_used = (ends[-1] // tile).astype(I32).reshape(1)
    order = jnp.argsort(e_flat, stable=True).astype(I32)
    sorted_start = jnp.cumsum(counts) - counts
    row = jnp.arange(n_rows, dtype=I32)
    row_e = jnp.repeat(tile_expert, tile)
    j = row - starts[row_e]
    src = order[jnp.clip(sorted_start[row_e] + j, 0, n_assign - 1)]
    row_token = jnp.where(j < counts[row_e], src // TOP_K, 0)
    return pos.astype(I32), row_token.astype(I32), tile_expert.astype(I32), n_used


def kernel(x_prompt, x_sample, cache_k, cache_v, state_gdn, state_conv, page_table, ln1_g, w_in, conv_w, a_log, dt_bias, gdn_norm_g, lambda_q1, lambda_k1, lambda_q2, lambda_k2, diff_norm_g, w_o, ln2_g, w_router, b_router, w_gate, b_gate, w_up, b_up, w_down, b_down, final_g):
    B, L, _ = x_prompt.shape
    DB, T, _ = x_sample.shape
    depth = w_in.shape[0]
    assert depth == 1
    l = 0
    n_p, n_s = B * L, DB * T
    lam_init = 0.8 - 0.6 * math.exp(-0.3 * l)
    s1 = jnp.sum(lambda_q1[l].astype(F32) * lambda_k1[l].astype(F32))
    s2 = jnp.sum(lambda_q2[l].astype(F32) * lambda_k2[l].astype(F32))
    lam = (jnp.exp(s1) - jnp.exp(s2) + lam_init).reshape(1, 1).astype(F32)

    x_p = x_prompt.reshape(n_p, D_MODEL)
    x_s = x_sample.reshape(n_s, D_MODEL)
    wl = w_in[l]
    o_z = CONV_DIM
    o_a = o_z + GROUP_W
    o_q = o_a + 2 * N_HEADS
    w_cat = jnp.concatenate(
        [wl[:, :o_a], wl[:, o_q:], wl[:, o_a:o_q],
         jnp.zeros((D_MODEL, LANES - 2 * N_HEADS), wl.dtype)], axis=1).astype(BF16)
    (qkv, z, ab, qdb, kdb, kd_p, kd_s, vd_p, vd_s, qt, vt) = _inproj(
        x_p, x_s, ln1_g[l].reshape(1, D_MODEL), w_cat, batch=B)

    gpar = jnp.zeros((8, LANES), F32)
    gpar = gpar.at[0, :N_HEADS].set(a_log[l].astype(F32)).at[1, :N_HEADS].set(dt_bias[l].astype(F32))
    norm_g = gdn_norm_g[l].reshape(1, HEAD_W).astype(F32)
    cw = conv_w[l].astype(F32)
    g_out_p, s_p = _gdn(
        qkv, ab, z, jnp.zeros((B, HIST_ROWS, CONV_DIM), F32),
        jnp.zeros((B, N_HEADS, HEAD_W, HEAD_W), F32), cw, gpar, norm_g,
        b=B, l=L, valid=GDN_CHUNK)

    def pad_tok(a):
        return jnp.pad(a.reshape(DB, T, a.shape[-1]), ((0, 0), (0, Q_ROWS - T), (0, 0)))

    def pad_tok2(a):
        return pad_tok(a).reshape(DB * Q_ROWS, a.shape[-1])

    qkv_s = qkv[n_p:]
    hist_s = jnp.pad(state_conv[l].astype(F32), ((0, 0), (HIST_ROWS - (CONV_W - 1), 0), (0, 0)))
    g_out_s, s_s = _gdn(
        pad_tok2(qkv_s), pad_tok2(ab[n_p:]), pad_tok2(z[n_p:]), hist_s,
        state_gdn[l].astype(F32), cw, gpar, norm_g, b=DB, l=Q_ROWS, valid=T)
    g_out_s = g_out_s.reshape(DB, Q_ROWS, GROUP_W)[:, :T].reshape(n_s, GROUP_W)

    dn_g = diff_norm_g[l].reshape(1, HEAD_W).astype(F32)
    d_out_p = _attn_prompt(lam, qt, kdb, vt, dn_g, batch=B, seq=L, tk=TOKEN_TILE,
                           out_scale=1.0 - lam_init)

    q_s = qdb[n_p:]
    col = jnp.arange(GROUP_W)
    sel = ((col[None, None, :] // HEAD_W == jnp.arange(N_HEADS)[None, :, None])
           & ((col[None, None, :] % HEAD_W) // DIFF_HALF == jnp.arange(2)[:, None, None]))
    qbd = jnp.where(sel[None, :, :, None, :], pad_tok(q_s)[:, None, None, :, :],
                    jnp.zeros((), BF16)).reshape(DB, QBD_ROWS, GROUP_W)
    n_pool = cache_k.shape[1]
    d_out_s = _attn_decode(
        page_table + l * n_pool, lam, qbd, pad_tok(kd_s), pad_tok(vd_s), dn_g,
        cache_k.reshape(-1, PAGE, GROUP_W), cache_v.reshape(-1, PAGE, GROUP_W),
        n_new=T, out_scale=1.0 - lam_init)
    d_out_s = d_out_s[:, :T].reshape(n_s, GROUP_W)

    w_r = jnp.pad(w_router[l].astype(F32), ((0, 0), (0, LANES - N_EXPERTS)))
    b_r = jnp.pad(b_router[l].astype(F32), (0, LANES - N_EXPERTS)).reshape(1, LANES)
    h, hn, gate, idx = _oproj_router(x_p, x_s, g_out_p, g_out_s, d_out_p, d_out_s,
                                     w_o[l].astype(BF16), ln2_g[l].reshape(1, D_MODEL), w_r, b_r)

    pos, row_token, tile_expert, n_used = _route(idx[:, :TOP_K], MOE_TILE)
    xs = jnp.take(hn, row_token, axis=0)
    ys = _moe_experts(tile_expert, n_used, xs, w_gate[l], b_gate[l][:, None, :],
                      w_up[l], b_up[l][:, None, :], w_down[l], b_down[l][:, None, :])

    yg = jnp.take(ys, pos.reshape(n_p + n_s, TOP_K).T, axis=0)
    y_p, y_s = _combine_norm(h, yg, gate, final_g.reshape(1, D_MODEL).astype(F32), n_p=n_p)

    dt_k, dt_v = cache_k.dtype, cache_v.dtype
    conv_p = jnp.stack([qkv[(i + 1) * L - (CONV_W - 1):(i + 1) * L] for i in range(B)])
    conv_s = qkv_s.reshape(DB, T, CONV_DIM)[:, T - (CONV_W - 1):]
    return (y_p.reshape(B, L, D_MODEL),
            y_s.reshape(DB, T, D_MODEL),
            kd_p.reshape(1, B, L, N_HEADS, HEAD_W).astype(dt_k),
            vd_p.reshape(1, B, L, N_HEADS, HEAD_W).astype(dt_v),
            kd_s.reshape(1, DB, T, N_HEADS, HEAD_W).astype(dt_k),
            vd_s.reshape(1, DB, T, N_HEADS, HEAD_W).astype(dt_v),
            s_p[None].astype(state_gdn.dtype),
            conv_p[None].astype(state_conv.dtype),
            s_s[None].astype(state_gdn.dtype),
            conv_s[None].astype(state_conv.dtype))
```

```python
import functools
import math

import jax
import jax.numpy as jnp
from jax import lax
from jax.experimental import pallas as pl
from jax.experimental.pallas import tpu as pltpu

F32 = jnp.float32
BF16 = jnp.bfloat16
I32 = jnp.int32

D_MODEL = 1024
HEAD_W = 128
N_HEADS = 4
GROUP_W = N_HEADS * HEAD_W
CONV_W = 4
CONV_DIM = 3 * GROUP_W
GDN_CHUNK = 64
DIFF_HALF = HEAD_W // 2
PAGE = 128
N_EXPERTS = 32
TOP_K = 4
SWIGLU_LIMIT = 7.0
SWIGLU_ALPHA = 1.702
EPS = 1e-6
LANES = 128
SUBLANES = 8
LOG2E = 1.4426950408889634
HIST_ROWS = 8
D_IN_PAD = CONV_DIM + 4 * GROUP_W + LANES
VMEM_LIMIT = 56 * 1024 * 1024


def _pick_tile(n, pref, mult=16):
    t = min(pref, n)
    while t > mult and (n % t or t % mult):
        t -= mult
    assert n % t == 0, (n, pref)
    return t


def _dot_nt(a, b):
    return lax.dot_general(a.astype(BF16), b.astype(BF16), (((1,), (1,)), ((), ())),
                           preferred_element_type=F32)


def _split2s(x):
    hi = x.astype(BF16)
    return hi, (x - hi.astype(F32)).astype(BF16)


def _dot3s(a, b, dot=None):
    dot = dot or (lambda p, q: jnp.dot(p, q, preferred_element_type=F32))
    return dot(a[0], b[0]) + (dot(a[0], b[1]) + dot(a[1], b[0]))


def _sigmoid(x):
    return 1.0 / (1.0 + jnp.exp(-x))


TOKEN_TILE = 512


def _prompt_spec(w, npt):
    return pl.BlockSpec((TOKEN_TILE, w), lambda i: (jnp.minimum(i, npt - 1), 0))


def _sample_spec(w, npt):
    return pl.BlockSpec((TOKEN_TILE, w), lambda i: (jnp.maximum(i - npt, 0), 0))


def _inproj_body(xp_ref, xs_ref, g_ref, w_ref, qkv_ref, z_ref, ab_ref, qdb_ref, kdb_ref,
                 kdp_ref, kds_ref, vdp_ref, vds_ref, qt_ref, vt_ref, *, npt):
    i = pl.program_id(0)
    x = jnp.where(i < npt, xp_ref[...], xs_ref[...])
    ms = jnp.mean(x * x, axis=-1, keepdims=True)
    xn = (x * lax.rsqrt(ms + EPS) * g_ref[...]).astype(BF16)

    def seg(lo, hi):
        return jnp.dot(xn, w_ref[:, lo:hi], preferred_element_type=F32)

    o = CONV_DIM
    qkv_ref[...] = seg(0, o)
    z_ref[...] = seg(o, o + GROUP_W)
    ab_ref[...] = seg(o + 4 * GROUP_W, o + 4 * GROUP_W + LANES)
    qd = seg(o + GROUP_W, o + 2 * GROUP_W) * (DIFF_HALF ** -0.5 * LOG2E)
    qdb_ref[...] = qd.astype(BF16)
    kd = seg(o + 2 * GROUP_W, o + 3 * GROUP_W)
    kdb_ref[...] = kd.astype(BF16)
    vd = seg(o + 3 * GROUP_W, o + 4 * GROUP_W)

    def store_heads(ref, val):
        for h in range(N_HEADS):
            ref[pl.ds(h, TOKEN_TILE, stride=N_HEADS), :] = val[:, h * HEAD_W:(h + 1) * HEAD_W]

    @pl.when(i < npt)
    def _prompt():
        store_heads(kdp_ref, kd)
        store_heads(vdp_ref, vd)
        qt_ref[...] = qd.T.astype(BF16)
        vt_ref[...] = vd.T.astype(BF16).reshape(N_HEADS, HEAD_W, TOKEN_TILE)

    @pl.when(i >= npt)
    def _sample():
        store_heads(kds_ref, kd)
        store_heads(vds_ref, vd)


def _inproj(x_p, x_s, ln_g, w_cat, *, batch):
    n_p, n_s = x_p.shape[0], x_s.shape[0]
    tm = TOKEN_TILE
    assert n_p % (batch * tm) == 0 and n_s % tm == 0
    npt, nst = n_p // tm, n_s // tm
    nkb = npt // batch
    n = n_p + n_s
    row = lambda w: pl.BlockSpec((tm, w), lambda i: (i, 0))
    full = lambda a: pl.BlockSpec(a.shape, lambda i: (0,) * a.ndim)

    def vt_map(i):
        t = jnp.minimum(i, npt - 1)
        return (t // nkb, 0, t % nkb, 0, 0)

    heads_p = pl.BlockSpec((tm * N_HEADS, HEAD_W), lambda i: (jnp.minimum(i, npt - 1), 0))
    heads_s = pl.BlockSpec((tm * N_HEADS, HEAD_W), lambda i: (jnp.maximum(i - npt, 0), 0))

    sds = jax.ShapeDtypeStruct
    return pl.pallas_call(
        functools.partial(_inproj_body, npt=npt),
        grid=(npt + nst,),
        in_specs=[_prompt_spec(D_MODEL, npt), _sample_spec(D_MODEL, npt), full(ln_g), full(w_cat)],
        out_specs=[row(CONV_DIM), row(GROUP_W), row(LANES), row(GROUP_W), row(GROUP_W),
                   heads_p, heads_s, heads_p, heads_s,
                   pl.BlockSpec((GROUP_W, tm), lambda i: (0, jnp.minimum(i, npt - 1))),
                   pl.BlockSpec((None, N_HEADS, None, HEAD_W, tm), vt_map)],
        out_shape=[sds((n, CONV_DIM), F32), sds((n, GROUP_W), F32), sds((n, LANES), F32),
                   sds((n, GROUP_W), BF16), sds((n, GROUP_W), BF16),
                   sds((n_p * N_HEADS, HEAD_W), F32), sds((n_s * N_HEADS, HEAD_W), F32),
                   sds((n_p * N_HEADS, HEAD_W), F32), sds((n_s * N_HEADS, HEAD_W), F32),
                   sds((GROUP_W, n_p), BF16),
                   sds((batch, N_HEADS, nkb, HEAD_W, tm), BF16)],
        compiler_params=pltpu.CompilerParams(
            dimension_semantics=("arbitrary",), vmem_limit_bytes=VMEM_LIMIT),
        name="inproj",
    )(x_p, x_s, ln_g, w_cat)


def _block_unit_lower_inverse(a, eye, blk):
    t = eye - a
    p = a
    n = 1
    while 2 * n < blk:
        ps = _split2s(p)
        p = _dot3s(ps, ps)
        t = t + _dot3s(_split2s(t), _split2s(p))
        n *= 2
    return t


def _gdn_stack_body(*refs, nspec, spp, cin, valid, seq_per_stack):
    qkv_refs, ab_refs, z_refs = refs[:nspec], refs[nspec:2 * nspec], refs[2 * nspec:3 * nspec]
    hist0_ref, s0_ref, convw_ref, gpar_ref, ng_ref = refs[3 * nspec:3 * nspec + 5]
    out_refs = refs[3 * nspec + 5:4 * nspec + 5]
    sfin_ref, s_scr, hist_scr = refs[4 * nspec + 5:]
    nseq = nspec * spp
    blk = cin
    c = pl.program_id(1)

    @pl.when(c == 0)
    def _init():
        s_scr[...] = s0_ref[...]
        hist_scr[...] = hist0_ref[...]

    w = convw_ref[...]
    h0 = HIST_ROWS - (CONV_W - 1)
    lane = lax.broadcasted_iota(I32, (cin, LANES), 1)
    rowi = lax.broadcasted_iota(I32, (cin, LANES), 0)
    acts, gbs, zs = [], [], []
    for s in range(nseq):
        ri, off = s // spp, (s % spp) * cin
        x = qkv_refs[ri][off:off + cin, :]
        xp = jnp.concatenate([hist_scr[s], x], axis=0)
        conv = xp[h0:h0 + cin] * w[0:1]
        for j in range(1, CONV_W):
            conv = conv + xp[h0 + j:h0 + j + cin] * w[j:j + 1]
        hist_scr[s] = x[cin - HIST_ROWS:cin]
        acts.append(conv * _sigmoid(conv))
        t = ab_refs[ri][off:off + cin, :]
        sp_in = t + gpar_ref[1:2, :]
        softplus = jnp.maximum(sp_in, 0.0) + jnp.log(1.0 + jnp.exp(-jnp.abs(sp_in)))
        gb = jnp.where(lane < N_HEADS, -jnp.exp(gpar_ref[0:1, :]) * softplus, _sigmoid(t))
        if valid < cin:
            gb = jnp.where(rowi < valid, gb, 0.0)
        gbs.append(gb)
        zs.append(z_refs[ri][off:off + cin, :])

    R = seq_per_stack * N_HEADS * blk
    shift = blk.bit_length() - 1
    r = lax.broadcasted_iota(I32, (R, R), 0)
    q_ = lax.broadcasted_iota(I32, (R, R), 1)
    same = (r >> shift) == (q_ >> shift)
    incl = same & (r >= q_)
    strict = same & (r > q_)
    eye = jnp.where(r == q_, 1.0, 0.0).astype(F32)
    tril = jnp.where(incl, 1.0, 0.0).astype(BF16)
    nrep = max(R // LANES, 1)

    def l2n(t):
        return t * lax.rsqrt(jnp.sum(t * t, axis=-1, keepdims=True) + EPS)

    def tn(p, q):
        return lax.dot_general(p, q, (((0,), (0,)), ((), ())), preferred_element_type=F32)

    for st in range(nseq // seq_per_stack):
        chains = [(s, h) for s in range(st * seq_per_stack, (st + 1) * seq_per_stack)
                  for h in range(N_HEADS)]

        def stack(fn):
            return jnp.concatenate([fn(s, h) for s, h in chains], axis=0)

        q = stack(lambda s, h: l2n(acts[s][:, h * HEAD_W:(h + 1) * HEAD_W])) * (HEAD_W ** -0.5)
        k = stack(lambda s, h: l2n(acts[s][:, GROUP_W + h * HEAD_W:GROUP_W + (h + 1) * HEAD_W]))
        v = stack(lambda s, h: acts[s][:, 2 * GROUP_W + h * HEAD_W:2 * GROUP_W + (h + 1) * HEAD_W])
        g = stack(lambda s, h: jnp.broadcast_to(gbs[s][:, h:h + 1], (blk, HEAD_W)))
        beta = stack(lambda s, h: jnp.broadcast_to(
            gbs[s][:, N_HEADS + h:N_HEADS + h + 1], (blk, HEAD_W)))
        g_hi = g.astype(BF16)
        g_r1 = g - g_hi.astype(F32)
        g_mid = g_r1.astype(BF16)
        g_lo = (g_r1 - g_mid.astype(F32)).astype(BF16)
        gam = (jnp.dot(tril, g_hi, preferred_element_type=F32)
               + jnp.dot(tril, g_mid, preferred_element_type=F32)
               + jnp.dot(tril, g_lo, preferred_element_type=F32))
        gcol = _rep_lanes(gam, nrep)[:, :R]
        grow = jnp.broadcast_to(gam.T[0:1, :], (R, R))
        dec = jnp.where(incl, jnp.exp(jnp.minimum(gcol - grow, 0.0)), 0.0)
        eg = jnp.exp(gam)
        kb = k.astype(BF16)
        kk = lax.dot_general(kb, kb, (((1,), (1,)), ((), ())), preferred_element_type=F32)
        bcol = _rep_lanes(beta, nrep)[:, :R]
        a_mat = jnp.where(strict, bcol * kk * dec, 0.0)
        t_inv = _block_unit_lower_inverse(a_mat, eye, blk)
        rhs = jnp.concatenate([beta * v, beta * k * eg], axis=1)
        sol = _dot3s(_split2s(t_inv), _split2s(rhs))
        u, w_ = sol[:, :HEAD_W], sol[:, HEAD_W:]
        ws, qs = [], []
        for ci, (s, h) in enumerate(chains):
            rows = slice(ci * blk, (ci + 1) * blk)
            wq = jnp.concatenate([w_[rows], q[rows]], axis=0)
            wq_s = _dot3s(_split2s(wq), _split2s(s_scr[s, h]))
            ws.append(wq_s[:blk])
            qs.append(wq_s[blk:])
        delta = u - jnp.concatenate(ws, axis=0)
        qk = lax.dot_general(q.astype(BF16), kb, (((1,), (1,)), ((), ())),
                             preferred_element_type=F32) * dec
        deltas = _split2s(delta)
        o = eg * jnp.concatenate(qs, axis=0) + _dot3s(_split2s(qk), deltas)
        glast = jnp.concatenate(
            [jnp.broadcast_to(gam[(ci + 1) * blk - 1:(ci + 1) * blk, :], (blk, HEAD_W))
             for ci in range(len(chains))], axis=0)
        kts = _split2s(k * jnp.exp(glast - gam))
        on = o * lax.rsqrt(jnp.mean(o * o, axis=-1, keepdims=True) + EPS) * ng_ref[...]
        for ci, (s, h) in enumerate(chains):
            rows = slice(ci * blk, (ci + 1) * blk)
            e_last = jnp.exp(jnp.broadcast_to(gam[(ci + 1) * blk - 1:(ci + 1) * blk, :],
                                              (HEAD_W, HEAD_W)))
            upd = _dot3s((kts[0][rows], kts[1][rows]), (deltas[0][rows], deltas[1][rows]), dot=tn)
            s_scr[s, h] = e_last * s_scr[s, h] + upd
            zh = zs[s][:, h * HEAD_W:(h + 1) * HEAD_W]
            ri, off = s // spp, (s % spp) * cin
            out_refs[ri][off:off + cin, h * HEAD_W:(h + 1) * HEAD_W] = on[rows] * (zh * _sigmoid(zh))

    @pl.when(c == pl.num_programs(1) - 1)
    def _fin():
        sfin_ref[...] = s_scr[...]


def _gdn_stacked(qkv, ab, z, hist0, s0, conv_w, gpar, norm_g, *, n_seq, l, cin, valid, nspec, spp):
    per_step = nspec * spp
    assert l % cin == 0 and n_seq % per_step == 0 and (spp == 1 or l == cin)
    assert cin % HIST_ROWS == 0 and cin & (cin - 1) == 0
    nc = l // cin
    ng = n_seq // per_step

    def tok(w, i):
        return pl.BlockSpec((spp * cin, w), lambda g, c: ((g * nspec + i) * nc + c, 0))

    per_g = lambda a: pl.BlockSpec((per_step,) + a.shape[1:],
                                   lambda g, c: (g,) + (0,) * (a.ndim - 1))
    full = lambda a: pl.BlockSpec(a.shape, lambda g, c: (0,) * a.ndim)
    toks = lambda w: [tok(w, i) for i in range(nspec)]
    out_tok = pl.BlockSpec((spp * cin, GROUP_W), lambda g, c: (g * nc + c, 0))
    return pl.pallas_call(
        functools.partial(_gdn_stack_body, nspec=nspec, spp=spp, cin=cin, valid=valid,
                          seq_per_stack=spp),
        grid=(ng, nc),
        in_specs=toks(CONV_DIM) + toks(LANES) + toks(GROUP_W)
                 + [per_g(hist0), per_g(s0), full(conv_w), full(gpar), full(norm_g)],
        out_specs=[out_tok] * nspec + [per_g(s0)],
        out_shape=[jax.ShapeDtypeStruct((ng * spp * l, GROUP_W), F32)] * nspec
                  + [jax.ShapeDtypeStruct(s0.shape, F32)],
        scratch_shapes=[pltpu.VMEM((per_step, N_HEADS, HEAD_W, HEAD_W), F32),
                        pltpu.VMEM((per_step, HIST_ROWS, CONV_DIM), F32)],
        compiler_params=pltpu.CompilerParams(
            dimension_semantics=("parallel", "arbitrary"), vmem_limit_bytes=VMEM_LIMIT),
        name="gdn",
    )(*([qkv] * nspec), *([ab] * nspec), *([z] * nspec), hist0, s0, conv_w, gpar, norm_g)


def _rep_lanes(x, n):
    return x if n == 1 else jnp.concatenate([x] * n, axis=1)


def _sublane_all(x, op):
    for shift in (4, 2, 1):
        x = op(x, pltpu.roll(x, shift, axis=0))
    return x


def _attn_prompt_body(lam_ref, qt_ref, k_ref, vt_ref, ng_ref, o_ref, m_scr, l_scr, acc_scr,
                      *, tq, tk, out_scale):
    qi = pl.program_id(2)
    qt = qt_ref[...]
    row = lax.broadcasted_iota(I32, qt.shape, 0)
    zero = jnp.zeros_like(qt)
    qqt = jnp.concatenate([jnp.where(row < DIFF_HALF, qt, zero),
                           jnp.where(row >= DIFF_HALF, qt, zero)], axis=1)
    m_scr[...] = jnp.full(m_scr.shape, -jnp.inf, F32)
    l_scr[...] = jnp.zeros(l_scr.shape, F32)
    acc_scr[...] = jnp.zeros(acc_scr.shape, F32)
    nq2 = 2 * tq

    def block(kb, masked):
        start = pl.multiple_of(kb * tk, tk)
        k = k_ref[pl.ds(start, tk), :]
        s = jnp.dot(k, qqt, preferred_element_type=F32)
        if masked:
            key = lax.broadcasted_iota(I32, s.shape, 0) + (kb * tk - qi * tq)
            qc = lax.broadcasted_iota(I32, s.shape, 1)
            qc = jnp.where(qc >= tq, qc - tq, qc)
            s = jnp.where(key <= qc, s, -jnp.inf)
        s3 = s.reshape(tk // SUBLANES, SUBLANES, nq2)
        m_prev = m_scr[...]
        m_new = jnp.maximum(m_prev, _sublane_all(jnp.max(s3, axis=0), jnp.maximum))
        alpha = jnp.exp2(m_prev - m_new)
        p3 = jnp.exp2(s3 - m_new[None])
        l_scr[...] = alpha * l_scr[...] + _sublane_all(jnp.sum(p3, axis=0), jnp.add)
        pv = jnp.dot(vt_ref[kb], p3.reshape(tk, nq2).astype(BF16),
                     preferred_element_type=F32)
        acc = acc_scr[...].reshape(HEAD_W // SUBLANES, SUBLANES, nq2) * alpha[None]
        acc_scr[...] = acc.reshape(HEAD_W, nq2) + pv
        m_scr[...] = m_new

    def body(kb, carry):
        block(kb, False)
        return carry

    def body_masked(kb, carry):
        block(kb, True)
        return carry

    n_full = (qi * tq) // tk
    n_all = (qi * tq + tq + tk - 1) // tk
    lax.fori_loop(0, n_full, body, 0)
    lax.fori_loop(n_full, n_all, body_masked, 0)

    ot = acc_scr[...] / l_scr[0:1, :]
    ot = ot[:, :tq] - lam_ref[0, 0] * ot[:, tq:]
    o = ot.T
    o = o * lax.rsqrt(jnp.mean(o * o, axis=-1, keepdims=True) + EPS) * ng_ref[...]
    o_ref[...] = o * out_scale


def _attn_prompt(lam, qt, kdb, vt, norm_g, *, batch, seq, tk, out_scale):
    tq = _pick_tile(seq, 512, LANES)
    nq = seq // tq
    return pl.pallas_call(
        functools.partial(_attn_prompt_body, tq=tq, tk=tk, out_scale=out_scale),
        grid=(batch, N_HEADS, nq),
        in_specs=[pl.BlockSpec(memory_space=pltpu.SMEM),
                  pl.BlockSpec((HEAD_W, tq), lambda b, h, i: (h, b * nq + i)),
                  pl.BlockSpec((seq, HEAD_W), lambda b, h, i: (b, h)),
                  pl.BlockSpec((None, None, seq // tk, HEAD_W, tk),
                               lambda b, h, i: (b, h, 0, 0, 0)),
                  pl.BlockSpec((1, HEAD_W), lambda b, h, i: (0, 0))],
        out_specs=pl.BlockSpec((tq, HEAD_W), lambda b, h, i: (b * nq + i, h)),
        out_shape=jax.ShapeDtypeStruct((batch * seq, GROUP_W), F32),
        scratch_shapes=[pltpu.VMEM((SUBLANES, 2 * tq), F32), pltpu.VMEM((SUBLANES, 2 * tq), F32),
                        pltpu.VMEM((HEAD_W, 2 * tq), F32)],
        compiler_params=pltpu.CompilerParams(
            dimension_semantics=("parallel", "parallel", "arbitrary"),
            vmem_limit_bytes=VMEM_LIMIT),
        name="attn_prompt",
    )(lam, qt, kdb, vt, norm_g)


Q_ROWS = 8
QH_ROWS = 2 * Q_ROWS
QS_ROWS = N_HEADS * QH_ROWS
DECODE_PAGES_PER_STEP = 8


def _attn_decode_body(pt_ref, lam_ref, q_ref, kn_ref, vn_ref, ng_ref, *rest,
                      pages_per_step, n_new, out_scale):
    G = pages_per_step
    k_refs, v_refs = rest[:G], rest[G:2 * G]
    o_ref, m_scr, l_scr, acc_scr = rest[2 * G:]
    j = pl.program_id(1)

    @pl.when(j == 0)
    def _init():
        m_scr[...] = jnp.full(m_scr.shape, -jnp.inf, F32)
        l_scr[...] = jnp.zeros(l_scr.shape, F32)
        acc_scr[...] = jnp.zeros(acc_scr.shape, F32)

    q = q_ref[0]
    lane = lax.broadcasted_iota(I32, (Q_ROWS, HEAD_W), 1)
    zero = jnp.zeros((Q_ROWS, HEAD_W), BF16)
    qh = []
    for h in range(N_HEADS):
        t = q[:, h * HEAD_W:(h + 1) * HEAD_W]
        qh.append(jnp.concatenate([jnp.where(lane < DIFF_HALF, t, zero),
                                   jnp.where(lane >= DIFF_HALF, t, zero)], axis=0))

    def update(ks, vs, mask):
        n = len(ks)
        s = jnp.concatenate(
            [jnp.concatenate([_dot_nt(qh[h], ks[g][h]) for g in range(n)], axis=1)
             for h in range(N_HEADS)], axis=0)
        if mask is not None:
            s = jnp.where(mask, s, -jnp.inf)
        m_prev = m_scr[...]
        m_new = jnp.maximum(m_prev, jnp.max(s, axis=1, keepdims=True))
        alpha = jnp.exp2(m_prev - m_new)
        p = jnp.exp2(s - _rep_lanes(m_new, n)).astype(BF16)
        l_scr[...] = alpha * l_scr[...] + jnp.sum(p.astype(F32), axis=1, keepdims=True)
        pv = []
        for h in range(N_HEADS):
            ph = p[h * QH_ROWS:(h + 1) * QH_ROWS]
            acc = jnp.dot(ph[:, :PAGE], vs[0][h], preferred_element_type=F32)
            for g in range(1, n):
                acc = acc + jnp.dot(ph[:, g * PAGE:(g + 1) * PAGE], vs[g][h],
                                    preferred_element_type=F32)
            pv.append(acc)
        acc_scr[...] = alpha * acc_scr[...] + jnp.concatenate(pv, axis=0)
        m_scr[...] = m_new

    heads = lambda ref: [ref[pl.ds(h, PAGE, stride=N_HEADS), :].astype(BF16)
                         for h in range(N_HEADS)]
    update([heads(k_refs[g]) for g in range(G)], [heads(v_refs[g]) for g in range(G)], None)

    @pl.when(j == pl.num_programs(1) - 1)
    def _fin():
        pad = jnp.zeros((PAGE - Q_ROWS, GROUP_W), F32)
        kn = jnp.concatenate([kn_ref[0], pad], axis=0).astype(BF16)
        vn = jnp.concatenate([vn_ref[0], pad], axis=0).astype(BF16)
        split = lambda a: [a[:, h * HEAD_W:(h + 1) * HEAD_W] for h in range(N_HEADS)]
        t = lax.broadcasted_iota(I32, (QS_ROWS, PAGE), 0) & (Q_ROWS - 1)
        cidx = lax.broadcasted_iota(I32, (QS_ROWS, PAGE), 1)
        update([split(kn)], [split(vn)], (cidx <= t) & (cidx < n_new))
        o = acc_scr[...] / l_scr[...]
        lam = lam_ref[0, 0]
        for h in range(N_HEADS):
            r0 = h * QH_ROWS
            oh = o[r0:r0 + Q_ROWS] - lam * o[r0 + Q_ROWS:r0 + QH_ROWS]
            oh = oh * lax.rsqrt(jnp.mean(oh * oh, axis=-1, keepdims=True) + EPS) * ng_ref[...]
            o_ref[0, :, h * HEAD_W:(h + 1) * HEAD_W] = oh * out_scale


def _attn_decode(page_table, lam, q, kn, vn, norm_g, cache_k, cache_v, *, layer, n_new,
                 out_scale):
    db, n_pages = page_table.shape
    G = DECODE_PAGES_PER_STEP
    while n_pages % G:
        G //= 2
    nj = n_pages // G
    pt_flat = page_table.reshape(-1)

    def page_map(g):
        return lambda b, j, pt: (layer, pt[b * n_pages + j * G + g], 0, 0)

    per_b = lambda a: pl.BlockSpec((1,) + a.shape[1:], lambda b, j, pt: (b, 0, 0))
    page_spec = lambda g: pl.BlockSpec((None, None, PAGE * N_HEADS, HEAD_W), page_map(g))
    as_rows = lambda c: c.reshape(c.shape[0], c.shape[1], PAGE * N_HEADS, HEAD_W)
    grid_spec = pltpu.PrefetchScalarGridSpec(
        num_scalar_prefetch=1,
        grid=(db, nj),
        in_specs=[pl.BlockSpec(memory_space=pltpu.SMEM), per_b(q), per_b(kn), per_b(vn),
                  pl.BlockSpec((1, HEAD_W), lambda b, j, pt: (0, 0))]
                 + [page_spec(g) for g in range(G)] + [page_spec(g) for g in range(G)],
        out_specs=pl.BlockSpec((1, Q_ROWS, GROUP_W), lambda b, j, pt: (b, 0, 0)),
        scratch_shapes=[pltpu.VMEM((QS_ROWS, LANES), F32), pltpu.VMEM((QS_ROWS, LANES), F32),
                        pltpu.VMEM((QS_ROWS, HEAD_W), F32)],
    )
    return pl.pallas_call(
        functools.partial(_attn_decode_body, pages_per_step=G, n_new=n_new,
                          out_scale=out_scale),
        grid_spec=grid_spec,
        out_shape=jax.ShapeDtypeStruct((db, Q_ROWS, GROUP_W), F32),
        compiler_params=pltpu.CompilerParams(
            dimension_semantics=("parallel", "arbitrary"), vmem_limit_bytes=VMEM_LIMIT),
        name="attn_decode",
    )(pt_flat, lam, q, kn, vn, norm_g, *([as_rows(cache_k)] * G), *([as_rows(cache_v)] * G))


def _oproj_router_body(xp_ref, xs_ref, gop_ref, gos_ref, dop_ref, dos_ref, wo_ref, ln_ref,
                       wr_ref, br_ref, h_ref, hn_ref, gate_ref, idx_ref, *, npt):
    is_p = pl.program_id(0) < npt
    x = jnp.where(is_p, xp_ref[...], xs_ref[...])
    go = jnp.where(is_p, gop_ref[...], gos_ref[...])
    do = jnp.where(is_p, dop_ref[...], dos_ref[...])
    h = (x
         + jnp.dot(go.astype(BF16), wo_ref[0:GROUP_W, :], preferred_element_type=F32)
         + jnp.dot(do.astype(BF16), wo_ref[GROUP_W:, :], preferred_element_type=F32))
    h_ref[...] = h
    hn = h * lax.rsqrt(jnp.mean(h * h, axis=-1, keepdims=True) + EPS) * ln_ref[...]
    hn_ref[...] = hn
    logits = _dot3s(_split2s(hn), _split2s(wr_ref[...])) + br_ref[...]
    lane = lax.broadcasted_iota(I32, logits.shape, 1)
    lane_f = lane.astype(F32)
    cur = jnp.where(lane < N_EXPERTS, logits, -jnp.inf)
    vals, idxs = [], []
    for _ in range(TOP_K):
        m = jnp.max(cur, axis=1, keepdims=True)
        i = jnp.min(jnp.where(cur == m, lane_f, float(LANES)), axis=1, keepdims=True)
        vals.append(m)
        idxs.append(i)
        cur = jnp.where(lane_f == i, -jnp.inf, cur)
    es = [jnp.exp(v - vals[0]) for v in vals]
    den = es[0]
    for e in es[1:]:
        den = den + e
    gate = jnp.zeros(logits.shape, F32)
    idx = jnp.zeros(logits.shape, F32)
    for k in range(TOP_K):
        gate = jnp.where(lane == k, es[k] / den, gate)
        idx = jnp.where(lane == k, idxs[k], idx)
    gate_ref[...] = gate
    idx_ref[...] = idx.astype(I32)


def _oproj_router(x_p, x_s, go_p, go_s, do_p, do_s, w_o, ln_g, w_r, b_r):
    n_p, n_s = x_p.shape[0], x_s.shape[0]
    tm = TOKEN_TILE
    npt, nst = n_p // tm, n_s // tm
    n = n_p + n_s
    row = lambda w: pl.BlockSpec((tm, w), lambda i: (i, 0))
    full = lambda a: pl.BlockSpec(a.shape, lambda i: (0,) * a.ndim)
    outs = [(D_MODEL, F32), (D_MODEL, F32), (LANES, F32), (LANES, I32)]
    return pl.pallas_call(
        functools.partial(_oproj_router_body, npt=npt),
        grid=(npt + nst,),
        in_specs=[_prompt_spec(D_MODEL, npt), _sample_spec(D_MODEL, npt),
                  _prompt_spec(GROUP_W, npt), _sample_spec(GROUP_W, npt),
                  _prompt_spec(GROUP_W, npt), _sample_spec(GROUP_W, npt),
                  full(w_o), full(ln_g), full(w_r), full(b_r)],
        out_specs=[row(w) for w, _ in outs],
        out_shape=[jax.ShapeDtypeStruct((n, w), dt) for w, dt in outs],
        compiler_params=pltpu.CompilerParams(
            dimension_semantics=("parallel",), vmem_limit_bytes=VMEM_LIMIT),
        name="oproj_router",
    )(x_p, x_s, go_p, go_s, do_p, do_s, w_o, ln_g, w_r, b_r)


MOE_TILE = 256


def _moe_body(te_ref, nused_ref, xs_ref, wg_ref, wu_ref, wd_ref, bg_ref, bu_ref, bd_ref,
              ys_ref, wg_bf, wu_bf, wd_bf):
    i = pl.program_id(0)
    e = te_ref[i]
    prev = te_ref[jnp.maximum(i - 1, 0)]

    @pl.when((i == 0) | (e != prev))
    def _cast():
        wg_bf[...] = wg_ref[0].astype(BF16)
        wu_bf[...] = wu_ref[0].astype(BF16)
        wd_bf[...] = wd_ref[0].astype(BF16)

    @pl.when(i < nused_ref[0])
    def _compute():
        x = xs_ref[...].astype(BF16)
        gl = jnp.minimum(jnp.dot(x, wg_bf[...], preferred_element_type=F32) + bg_ref[0],
                         SWIGLU_LIMIT)
        lin = jnp.clip(jnp.dot(x, wu_bf[...], preferred_element_type=F32) + bu_ref[0],
                       -SWIGLU_LIMIT, SWIGLU_LIMIT)
        act = gl * _sigmoid(SWIGLU_ALPHA * gl) * (lin + 1.0)
        ys_ref[...] = jnp.dot(act.astype(BF16), wd_bf[...],
                              preferred_element_type=F32) + bd_ref[0]

    @pl.when(i >= nused_ref[0])
    def _idle():
        ys_ref[...] = jnp.zeros(ys_ref.shape, F32)


def _moe_experts(tile_expert, n_used, xs, w_gate, b_gate, w_up, b_up, w_down, b_down):
    r = xs.shape[0]
    tm = MOE_TILE
    d_ff = w_gate.shape[-1]
    wspec = lambda a: pl.BlockSpec((1,) + a.shape[1:], lambda i, te, nu: (te[i], 0, 0))
    grid_spec = pltpu.PrefetchScalarGridSpec(
        num_scalar_prefetch=2,
        grid=(r // tm,),
        in_specs=[pl.BlockSpec((tm, D_MODEL), lambda i, te, nu: (i, 0)),
                  wspec(w_gate), wspec(w_up), wspec(w_down),
                  wspec(b_gate), wspec(b_up), wspec(b_down)],
        out_specs=pl.BlockSpec((tm, D_MODEL), lambda i, te, nu: (i, 0)),
        scratch_shapes=[pltpu.VMEM((D_MODEL, d_ff), BF16), pltpu.VMEM((D_MODEL, d_ff), BF16),
                        pltpu.VMEM((d_ff, D_MODEL), BF16)],
    )
    return pl.pallas_call(
        _moe_body,
        grid_spec=grid_spec,
        out_shape=jax.ShapeDtypeStruct((r, D_MODEL), F32),
        compiler_params=pltpu.CompilerParams(
            dimension_semantics=("arbitrary",), vmem_limit_bytes=VMEM_LIMIT),
        name="moe_experts",
    )(tile_expert, n_used, xs, w_gate, w_up, w_down, b_gate, b_up, b_down)


def _combine_body(h_ref, yg_ref, gate_ref, fg_ref, yp_ref, ys_ref, *, npt):
    gate = gate_ref[...]
    out = h_ref[...]
    for k in range(TOP_K):
        out = out + gate[:, k:k + 1] * yg_ref[k]
    y = out * lax.rsqrt(jnp.mean(out * out, axis=-1, keepdims=True) + EPS) * fg_ref[...]
    i = pl.program_id(0)

    @pl.when(i < npt)
    def _prompt():
        yp_ref[...] = y

    @pl.when(i >= npt)
    def _sample():
        ys_ref[...] = y


def _combine_norm(h, yg, gate, final_g, *, n_p):
    n = h.shape[0]
    tm = TOKEN_TILE
    npt = n_p // tm
    row = lambda w: pl.BlockSpec((tm, w), lambda i: (i, 0))
    return pl.pallas_call(
        functools.partial(_combine_body, npt=npt),
        grid=(n // tm,),
        in_specs=[row(D_MODEL), pl.BlockSpec((TOP_K, tm, D_MODEL), lambda i: (0, i, 0)),
                  row(LANES), pl.BlockSpec((1, D_MODEL), lambda i: (0, 0))],
        out_specs=[_prompt_spec(D_MODEL, npt), _sample_spec(D_MODEL, npt)],
        out_shape=[jax.ShapeDtypeStruct((n_p, D_MODEL), F32),
                   jax.ShapeDtypeStruct((n - n_p, D_MODEL), F32)],
        compiler_params=pltpu.CompilerParams(
            dimension_semantics=("arbitrary",), vmem_limit_bytes=VMEM_LIMIT),
        name="combine_norm",
    )(h, yg, gate, final_g)


def _route(top_i, tile):
    n = top_i.shape[0]
    n_assign = n * TOP_K
    e_flat = top_i.reshape(-1)
    onehot = (e_flat[:, None] == jnp.arange(N_EXPERTS, dtype=I32)[None, :]).astype(I32)
    csum = jnp.cumsum(onehot, axis=0)
    rank = jnp.sum(onehot * csum, axis=1) - 1
    counts = csum[-1]
    padded = ((counts + tile - 1) // tile) * tile
    ends = jnp.cumsum(padded)
    starts = ends - padded
    pos = starts[e_flat] + rank
    n_rows = n_assign + N_EXPERTS * tile
    n_tiles = n_rows // tile
    tile_start = jnp.arange(n_tiles, dtype=I32) * tile
    tile_expert = jnp.minimum(
        jnp.sum((tile_start[:, None] >= ends[None, :]).astype(I32), axis=1), N_EXPERTS - 1)
    n_used = (ends[-1] // tile).astype(I32).reshape(1)
    order = jnp.argsort(e_flat, stable=True).astype(I32)
    sorted_start = jnp.cumsum(counts) - counts
    row = jnp.arange(n_rows, dtype=I32)
    row_e = jnp.repeat(tile_expert, tile)
    j = row - starts[row_e]
    src = order[jnp.clip(sorted_start[row_e] + j, 0, n_assign - 1)]
    row_token = jnp.where(j < counts[row_e], src // TOP_K, 0)
    return pos.astype(I32), row_token.astype(I32), tile_expert.astype(I32), n_used


def kernel(x_prompt, x_sample, cache_k, cache_v, state_gdn, state_conv, page_table, ln1_g, w_in, conv_w, a_log, dt_bias, gdn_norm_g, lambda_q1, lambda_k1, lambda_q2, lambda_k2, diff_norm_g, w_o, ln2_g, w_router, b_router, w_gate, b_gate, w_up, b_up, w_down, b_down, final_g):
    B, L, _ = x_prompt.shape
    DB, T, _ = x_sample.shape
    depth = w_in.shape[0]
    assert depth == 1
    l = 0
    n_p, n_s = B * L, DB * T
    lam_init = 0.8 - 0.6 * math.exp(-0.3 * l)
    s1 = jnp.sum(lambda_q1[l].astype(F32) * lambda_k1[l].astype(F32))
    s2 = jnp.sum(lambda_q2[l].astype(F32) * lambda_k2[l].astype(F32))
    lam = (jnp.exp(s1) - jnp.exp(s2) + lam_init).reshape(1, 1).astype(F32)

    x_p = x_prompt.reshape(n_p, D_MODEL)
    x_s = x_sample.reshape(n_s, D_MODEL)
    wl = w_in[l]
    o_z = CONV_DIM
    o_a = o_z + GROUP_W
    o_q = o_a + 2 * N_HEADS
    w_cat = jnp.concatenate(
        [wl[:, :o_a], wl[:, o_q:], wl[:, o_a:o_q],
         jnp.zeros((D_MODEL, LANES - 2 * N_HEADS), wl.dtype)], axis=1).astype(BF16)
    (qkv, z, ab, qdb, kdb, kd_p, kd_s, vd_p, vd_s, qt, vt) = _inproj(
        x_p, x_s, ln1_g[l].reshape(1, D_MODEL), w_cat, batch=B)

    gpar = jnp.zeros((8, LANES), F32)
    gpar = gpar.at[0, :N_HEADS].set(a_log[l].astype(F32)).at[1, :N_HEADS].set(dt_bias[l].astype(F32))
    norm_g = gdn_norm_g[l].reshape(1, HEAD_W).astype(F32)
    cw = conv_w[l].astype(F32)
    *g_outs, s_p = _gdn_stacked(
        qkv, ab, z, jnp.zeros((B, HIST_ROWS, CONV_DIM), F32),
        jnp.zeros((B, N_HEADS, HEAD_W, HEAD_W), F32), cw, gpar, norm_g,
        n_seq=B, l=L, cin=GDN_CHUNK, valid=GDN_CHUNK, nspec=B, spp=1)
    g_out_p = jnp.concatenate(g_outs, axis=0)

    def pad_tok(a):
        return jnp.pad(a.reshape(DB, T, a.shape[-1]), ((0, 0), (0, Q_ROWS - T), (0, 0)))

    def pad_tok2(a):
        return pad_tok(a).reshape(DB * Q_ROWS, a.shape[-1])

    qkv_s = qkv[n_p:]
    hist_s = jnp.pad(state_conv[l].astype(F32), ((0, 0), (HIST_ROWS - (CONV_W - 1), 0), (0, 0)))
    sps = 4 if DB % 4 == 0 else 1
    g_out_s, s_s = _gdn_stacked(
        pad_tok2(qkv_s), pad_tok2(ab[n_p:]), pad_tok2(z[n_p:]), hist_s,
        state_gdn[l].astype(F32), cw, gpar, norm_g,
        n_seq=DB, l=Q_ROWS, cin=Q_ROWS, valid=T, nspec=1, spp=sps)
    g_out_s = g_out_s.reshape(DB, Q_ROWS, GROUP_W)[:, :T].reshape(n_s, GROUP_W)

    dn_g = diff_norm_g[l].reshape(1, HEAD_W).astype(F32)
    d_out_p = _attn_prompt(lam, qt, kdb, vt, dn_g, batch=B, seq=L, tk=TOKEN_TILE,
                           out_scale=1.0 - lam_init)

    d_out_s = _attn_decode(
        page_table, lam, pad_tok(qdb[n_p:]), pad_tok(kd_s.reshape(n_s, GROUP_W)), pad_tok(vd_s.reshape(n_s, GROUP_W)), dn_g,
        cache_k, cache_v, layer=l, n_new=T, out_scale=1.0 - lam_init)
    d_out_s = d_out_s[:, :T].reshape(n_s, GROUP_W)

    w_r = jnp.pad(w_router[l].astype(F32), ((0, 0), (0, LANES - N_EXPERTS)))
    b_r = jnp.pad(b_router[l].astype(F32), (0, LANES - N_EXPERTS)).reshape(1, LANES)
    h, hn, gate, idx = _oproj_router(x_p, x_s, g_out_p, g_out_s, d_out_p, d_out_s,
                                     w_o[l].astype(BF16), ln2_g[l].reshape(1, D_MODEL), w_r, b_r)

    pos, row_token, tile_expert, n_used = _route(idx[:, :TOP_K], MOE_TILE)
    xs = jnp.take(hn, row_token, axis=0)
    ys = _moe_experts(tile_expert, n_used, xs, w_gate[l], b_gate[l][:, None, :],
                      w_up[l], b_up[l][:, None, :], w_down[l], b_down[l][:, None, :])

    yg = jnp.take(ys, pos.reshape(n_p + n_s, TOP_K).T, axis=0)
    y_p, y_s = _combine_norm(h, yg, gate, final_g.reshape(1, D_MODEL).astype(F32), n_p=n_p)

    dt_k, dt_v = cache_k.dtype, cache_v.dtype
    conv_p = jnp.stack([qkv[(i + 1) * L - (CONV_W - 1):(i + 1) * L] for i in range(B)])
    conv_s = qkv_s.reshape(DB, T, CONV_DIM)[:, T - (CONV_W - 1):]
    return (y_p.reshape(B, L, D_MODEL),
            y_s.reshape(DB, T, D_MODEL),
            kd_p.reshape(1, B, L, N_HEADS, HEAD_W).astype(dt_k),
            vd_p.reshape(1, B, L, N_HEADS, HEAD_W).astype(dt_v),
            kd_s.reshape(1, DB, T, N_HEADS, HEAD_W).astype(dt_k),
            vd_s.reshape(1, DB, T, N_HEADS, HEAD_W).astype(dt_v),
            s_p[None].astype(state_gdn.dtype),
            conv_p[None].astype(state_conv.dtype),
            s_s[None].astype(state_gdn.dtype),
            conv_s[None].astype(state_conv.dtype))
```

```python
import functools
import math

import jax
import jax.numpy as jnp
from jax import lax
from jax.experimental import pallas as pl
from jax.experimental.pallas import tpu as pltpu

F32 = jnp.float32
BF16 = jnp.bfloat16
I32 = jnp.int32

D_MODEL = 1024
HEAD_W = 128
N_HEADS = 4
GROUP_W = N_HEADS * HEAD_W
CONV_W = 4
CONV_DIM = 3 * GROUP_W
GDN_CHUNK = 64
DIFF_HALF = HEAD_W // 2
PAGE = 128
N_EXPERTS = 32
TOP_K = 4
SWIGLU_LIMIT = 7.0
SWIGLU_ALPHA = 1.702
EPS = 1e-6
LANES = 128
SUBLANES = 8
LOG2E = 1.4426950408889634
HIST_ROWS = 8
D_IN_PAD = CONV_DIM + 4 * GROUP_W + LANES
VMEM_LIMIT = 56 * 1024 * 1024


def _pick_tile(n, pref, mult=16):
    t = min(pref, n)
    while t > mult and (n % t or t % mult):
        t -= mult
    assert n % t == 0, (n, pref)
    return t


def _dot(a, b):
    return jnp.dot(a.astype(BF16), b.astype(BF16), preferred_element_type=F32)


def _dot_nt(a, b):
    return lax.dot_general(a.astype(BF16), b.astype(BF16), (((1,), (1,)), ((), ())),
                           preferred_element_type=F32)


def _split2s(x):
    hi = x.astype(BF16)
    return hi, (x - hi.astype(F32)).astype(BF16)


def _dot3s(a, b, dot=None):
    dot = dot or (lambda p, q: jnp.dot(p, q, preferred_element_type=F32))
    return dot(a[0], b[0]) + (dot(a[0], b[1]) + dot(a[1], b[0]))


def _sigmoid(x):
    return 1.0 / (1.0 + jnp.exp(-x))


TOKEN_TILE = 512


def _prompt_spec(w, npt):
    return pl.BlockSpec((TOKEN_TILE, w), lambda i: (jnp.minimum(i, npt - 1), 0))


def _sample_spec(w, npt):
    return pl.BlockSpec((TOKEN_TILE, w), lambda i: (jnp.maximum(i - npt, 0), 0))


def _inproj_body(xp_ref, xs_ref, g_ref, w_ref, qkv_ref, z_ref, ab_ref, qdb_ref, kdb_ref,
                 kdp_ref, kds_ref, vdp_ref, vds_ref, qt_ref, vt_ref, *, npt):
    i = pl.program_id(0)
    x = jnp.where(i < npt, xp_ref[...], xs_ref[...])
    ms = jnp.mean(x * x, axis=-1, keepdims=True)
    xn = (x * lax.rsqrt(ms + EPS) * g_ref[...]).astype(BF16)

    def seg(lo, hi):
        return jnp.dot(xn, w_ref[:, lo:hi], preferred_element_type=F32)

    o = CONV_DIM
    qkv_ref[...] = seg(0, o)
    z_ref[...] = seg(o, o + GROUP_W)
    ab_ref[...] = seg(o + 4 * GROUP_W, o + 4 * GROUP_W + LANES)
    qd = seg(o + GROUP_W, o + 2 * GROUP_W) * (DIFF_HALF ** -0.5 * LOG2E)
    qdb_ref[...] = qd.astype(BF16)
    kd = seg(o + 2 * GROUP_W, o + 3 * GROUP_W)
    kdb_ref[...] = kd.astype(BF16)
    vd = seg(o + 3 * GROUP_W, o + 4 * GROUP_W)

    def store_heads(ref, val):
        for h in range(N_HEADS):
            ref[pl.ds(h, TOKEN_TILE, stride=N_HEADS), :] = val[:, h * HEAD_W:(h + 1) * HEAD_W]

    @pl.when(i < npt)
    def _prompt():
        store_heads(kdp_ref, kd)
        store_heads(vdp_ref, vd)
        qt_ref[...] = qd.T.astype(BF16)
        vt_ref[...] = vd.T.astype(BF16).reshape(N_HEADS, HEAD_W, TOKEN_TILE)

    @pl.when(i >= npt)
    def _sample():
        store_heads(kds_ref, kd)
        store_heads(vds_ref, vd)


def _inproj(x_p, x_s, ln_g, w_cat, *, batch):
    n_p, n_s = x_p.shape[0], x_s.shape[0]
    tm = TOKEN_TILE
    assert n_p % (batch * tm) == 0 and n_s % tm == 0
    npt, nst = n_p // tm, n_s // tm
    nkb = npt // batch
    n = n_p + n_s
    row = lambda w: pl.BlockSpec((tm, w), lambda i: (i, 0))
    full = lambda a: pl.BlockSpec(a.shape, lambda i: (0,) * a.ndim)

    def vt_map(i):
        t = jnp.minimum(i, npt - 1)
        return (t // nkb, 0, t % nkb, 0, 0)

    heads_p = pl.BlockSpec((tm * N_HEADS, HEAD_W), lambda i: (jnp.minimum(i, npt - 1), 0))
    heads_s = pl.BlockSpec((tm * N_HEADS, HEAD_W), lambda i: (jnp.maximum(i - npt, 0), 0))

    sds = jax.ShapeDtypeStruct
    return pl.pallas_call(
        functools.partial(_inproj_body, npt=npt),
        grid=(npt + nst,),
        in_specs=[_prompt_spec(D_MODEL, npt), _sample_spec(D_MODEL, npt), full(ln_g), full(w_cat)],
        out_specs=[row(CONV_DIM), row(GROUP_W), row(LANES), row(GROUP_W), row(GROUP_W),
                   heads_p, heads_s, heads_p, heads_s,
                   pl.BlockSpec((GROUP_W, tm), lambda i: (0, jnp.minimum(i, npt - 1))),
                   pl.BlockSpec((None, N_HEADS, None, HEAD_W, tm), vt_map)],
        out_shape=[sds((n, CONV_DIM), F32), sds((n, GROUP_W), F32), sds((n, LANES), F32),
                   sds((n, GROUP_W), BF16), sds((n, GROUP_W), BF16),
                   sds((n_p * N_HEADS, HEAD_W), F32), sds((n_s * N_HEADS, HEAD_W), F32),
                   sds((n_p * N_HEADS, HEAD_W), F32), sds((n_s * N_HEADS, HEAD_W), F32),
                   sds((GROUP_W, n_p), BF16),
                   sds((batch, N_HEADS, nkb, HEAD_W, tm), BF16)],
        compiler_params=pltpu.CompilerParams(
            dimension_semantics=("arbitrary",), vmem_limit_bytes=VMEM_LIMIT),
        name="inproj",
    )(x_p, x_s, ln_g, w_cat)


def _block_unit_lower_inverse(a, eye, blk):
    t = eye - a
    p = a
    n = 1
    while 2 * n < blk:
        if n == 1:
            ps = _split2s(p)
            p = _dot3s(ps, ps)
            t = t + _dot3s(_split2s(t), _split2s(p))
        else:
            p = _dot(p, p)
            t = t + _dot(t, p)
        n *= 2
    return t


def _gdn_stack_body(*refs, nspec, spp, cin, valid, seq_per_stack):
    qkv_refs, ab_refs, z_refs = refs[:nspec], refs[nspec:2 * nspec], refs[2 * nspec:3 * nspec]
    hist0_ref, s0_ref, convw_ref, gpar_ref, ng_ref = refs[3 * nspec:3 * nspec + 5]
    out_refs = refs[3 * nspec + 5:4 * nspec + 5]
    sfin_ref, s_scr, hist_scr = refs[4 * nspec + 5:]
    nseq = nspec * spp
    blk = cin
    c = pl.program_id(1)

    @pl.when(c == 0)
    def _init():
        s_scr[...] = s0_ref[...]
        hist_scr[...] = hist0_ref[...]

    w = convw_ref[...]
    h0 = HIST_ROWS - (CONV_W - 1)
    lane = lax.broadcasted_iota(I32, (cin, LANES), 1)
    rowi = lax.broadcasted_iota(I32, (cin, LANES), 0)
    acts, gbs, zs = [], [], []
    for s in range(nseq):
        ri, off = s // spp, (s % spp) * cin
        x = qkv_refs[ri][off:off + cin, :]
        xp = jnp.concatenate([hist_scr[s], x], axis=0)
        conv = xp[h0:h0 + cin] * w[0:1]
        for j in range(1, CONV_W):
            conv = conv + xp[h0 + j:h0 + j + cin] * w[j:j + 1]
        hist_scr[s] = x[cin - HIST_ROWS:cin]
        acts.append(conv * _sigmoid(conv))
        t = ab_refs[ri][off:off + cin, :]
        sp_in = t + gpar_ref[1:2, :]
        softplus = jnp.maximum(sp_in, 0.0) + jnp.log(1.0 + jnp.exp(-jnp.abs(sp_in)))
        gb = jnp.where(lane < N_HEADS, -jnp.exp(gpar_ref[0:1, :]) * softplus, _sigmoid(t))
        if valid < cin:
            gb = jnp.where(rowi < valid, gb, 0.0)
        gbs.append(gb)
        zs.append(z_refs[ri][off:off + cin, :])

    R = seq_per_stack * N_HEADS * blk
    shift = blk.bit_length() - 1
    r = lax.broadcasted_iota(I32, (R, R), 0)
    q_ = lax.broadcasted_iota(I32, (R, R), 1)
    same = (r >> shift) == (q_ >> shift)
    incl = same & (r >= q_)
    strict = same & (r > q_)
    eye = jnp.where(r == q_, 1.0, 0.0).astype(F32)
    tril = jnp.where(incl, 1.0, 0.0).astype(BF16)
    nrep = max(R // LANES, 1)

    def l2n(t):
        return t * lax.rsqrt(jnp.sum(t * t, axis=-1, keepdims=True) + EPS)

    def tn(p, q):
        return lax.dot_general(p, q, (((0,), (0,)), ((), ())), preferred_element_type=F32)

    for st in range(nseq // seq_per_stack):
        chains = [(s, h) for s in range(st * seq_per_stack, (st + 1) * seq_per_stack)
                  for h in range(N_HEADS)]

        def stack(fn):
            return jnp.concatenate([fn(s, h) for s, h in chains], axis=0)

        q = stack(lambda s, h: l2n(acts[s][:, h * HEAD_W:(h + 1) * HEAD_W])) * (HEAD_W ** -0.5)
        k = stack(lambda s, h: l2n(acts[s][:, GROUP_W + h * HEAD_W:GROUP_W + (h + 1) * HEAD_W]))
        v = stack(lambda s, h: acts[s][:, 2 * GROUP_W + h * HEAD_W:2 * GROUP_W + (h + 1) * HEAD_W])
        g = stack(lambda s, h: jnp.broadcast_to(gbs[s][:, h:h + 1], (blk, HEAD_W)))
        beta = stack(lambda s, h: jnp.broadcast_to(
            gbs[s][:, N_HEADS + h:N_HEADS + h + 1], (blk, HEAD_W)))
        g_hi = g.astype(BF16)
        g_r1 = g - g_hi.astype(F32)
        g_mid = g_r1.astype(BF16)
        g_lo = (g_r1 - g_mid.astype(F32)).astype(BF16)
        gam = (jnp.dot(tril, g_hi, preferred_element_type=F32)
               + jnp.dot(tril, g_mid, preferred_element_type=F32)
               + jnp.dot(tril, g_lo, preferred_element_type=F32))
        gcol = _rep_lanes(gam, nrep)[:, :R]
        grow = jnp.broadcast_to(gam.T[0:1, :], (R, R))
        dec = jnp.where(incl, jnp.exp(jnp.minimum(gcol - grow, 0.0)), 0.0)
        eg = jnp.exp(gam)
        kb = k.astype(BF16)
        kk = lax.dot_general(kb, kb, (((1,), (1,)), ((), ())), preferred_element_type=F32)
        bcol = _rep_lanes(beta, nrep)[:, :R]
        a_mat = jnp.where(strict, bcol * kk * dec, 0.0)
        t_inv = _block_unit_lower_inverse(a_mat, eye, blk)
        rhs = jnp.concatenate([beta * v, beta * k * eg], axis=1)
        sol = _dot(t_inv, rhs)
        u, w_ = sol[:, :HEAD_W], sol[:, HEAD_W:]
        ws, qs = [], []
        for ci, (s, h) in enumerate(chains):
            rows = slice(ci * blk, (ci + 1) * blk)
            wq = jnp.concatenate([w_[rows], q[rows]], axis=0)
            wq_s = _dot(wq, s_scr[s, h])
            ws.append(wq_s[:blk])
            qs.append(wq_s[blk:])
        delta = u - jnp.concatenate(ws, axis=0)
        qk = lax.dot_general(q.astype(BF16), kb, (((1,), (1,)), ((), ())),
                             preferred_element_type=F32) * dec
        deltab = delta.astype(BF16)
        o = eg * jnp.concatenate(qs, axis=0) + jnp.dot(qk.astype(BF16), deltab,
                                                       preferred_element_type=F32)
        glast = jnp.concatenate(
            [jnp.broadcast_to(gam[(ci + 1) * blk - 1:(ci + 1) * blk, :], (blk, HEAD_W))
             for ci in range(len(chains))], axis=0)
        ktb = (k * jnp.exp(glast - gam)).astype(BF16)
        on = o * lax.rsqrt(jnp.mean(o * o, axis=-1, keepdims=True) + EPS) * ng_ref[...]
        for ci, (s, h) in enumerate(chains):
            rows = slice(ci * blk, (ci + 1) * blk)
            e_last = jnp.exp(jnp.broadcast_to(gam[(ci + 1) * blk - 1:(ci + 1) * blk, :],
                                              (HEAD_W, HEAD_W)))
            s_scr[s, h] = e_last * s_scr[s, h] + tn(ktb[rows], deltab[rows])
            zh = zs[s][:, h * HEAD_W:(h + 1) * HEAD_W]
            ri, off = s // spp, (s % spp) * cin
            out_refs[ri][off:off + cin, h * HEAD_W:(h + 1) * HEAD_W] = on[rows] * (zh * _sigmoid(zh))

    @pl.when(c == pl.num_programs(1) - 1)
    def _fin():
        sfin_ref[...] = s_scr[...]


def _gdn_stacked(qkv, ab, z, hist0, s0, conv_w, gpar, norm_g, *, n_seq, l, cin, valid, nspec, spp):
    per_step = nspec * spp
    assert l % cin == 0 and n_seq % per_step == 0 and (spp == 1 or l == cin)
    assert cin % HIST_ROWS == 0 and cin & (cin - 1) == 0
    nc = l // cin
    ng = n_seq // per_step

    def tok(w, i):
        return pl.BlockSpec((spp * cin, w), lambda g, c: ((g * nspec + i) * nc + c, 0))

    per_g = lambda a: pl.BlockSpec((per_step,) + a.shape[1:],
                                   lambda g, c: (g,) + (0,) * (a.ndim - 1))
    full = lambda a: pl.BlockSpec(a.shape, lambda g, c: (0,) * a.ndim)
    toks = lambda w: [tok(w, i) for i in range(nspec)]
    out_tok = pl.BlockSpec((spp * cin, GROUP_W), lambda g, c: (g * nc + c, 0))
    return pl.pallas_call(
        functools.partial(_gdn_stack_body, nspec=nspec, spp=spp, cin=cin, valid=valid,
                          seq_per_stack=spp),
        grid=(ng, nc),
        in_specs=toks(CONV_DIM) + toks(LANES) + toks(GROUP_W)
                 + [per_g(hist0), per_g(s0), full(conv_w), full(gpar), full(norm_g)],
        out_specs=[out_tok] * nspec + [per_g(s0)],
        out_shape=[jax.ShapeDtypeStruct((ng * spp * l, GROUP_W), F32)] * nspec
                  + [jax.ShapeDtypeStruct(s0.shape, F32)],
        scratch_shapes=[pltpu.VMEM((per_step, N_HEADS, HEAD_W, HEAD_W), F32),
                        pltpu.VMEM((per_step, HIST_ROWS, CONV_DIM), F32)],
        compiler_params=pltpu.CompilerParams(
            dimension_semantics=("parallel", "arbitrary"), vmem_limit_bytes=VMEM_LIMIT),
        name="gdn",
    )(*([qkv] * nspec), *([ab] * nspec), *([z] * nspec), hist0, s0, conv_w, gpar, norm_g)


ATTN_TQ = 1024


def _rep_lanes(x, n):
    return x if n == 1 else jnp.concatenate([x] * n, axis=1)


def _sublane_all(x, op):
    for shift in (4, 2, 1):
        x = op(x, pltpu.roll(x, shift, axis=0))
    return x


def _attn_prompt_body(lam_ref, qt_ref, k_ref, vt_ref, ng_ref, o_ref, m_scr, l_scr, acc_scr,
                      *, tq, tk, out_scale):
    qi = pl.program_id(2)
    qt = qt_ref[...]
    row = lax.broadcasted_iota(I32, qt.shape, 0)
    zero = jnp.zeros_like(qt)
    qqt = jnp.concatenate([jnp.where(row < DIFF_HALF, qt, zero),
                           jnp.where(row >= DIFF_HALF, qt, zero)], axis=1)
    m_scr[...] = jnp.full(m_scr.shape, -jnp.inf, F32)
    l_scr[...] = jnp.zeros(l_scr.shape, F32)
    acc_scr[...] = jnp.zeros(acc_scr.shape, F32)
    nq2 = 2 * tq

    def block(kb, masked):
        start = pl.multiple_of(kb * tk, tk)
        k = k_ref[pl.ds(start, tk), :]
        s = jnp.dot(k, qqt, preferred_element_type=F32)
        if masked:
            key = lax.broadcasted_iota(I32, s.shape, 0) + (kb * tk - qi * tq)
            qc = lax.broadcasted_iota(I32, s.shape, 1)
            qc = jnp.where(qc >= tq, qc - tq, qc)
            s = jnp.where(key <= qc, s, -jnp.inf)
        s3 = s.reshape(tk // SUBLANES, SUBLANES, nq2)
        m_prev = m_scr[...]
        m_new = jnp.maximum(m_prev, _sublane_all(jnp.max(s3, axis=0), jnp.maximum))
        alpha = jnp.exp2(m_prev - m_new)
        p3 = jnp.exp2(s3 - m_new[None])
        l_scr[...] = alpha * l_scr[...] + _sublane_all(jnp.sum(p3, axis=0), jnp.add)
        pv = jnp.dot(vt_ref[kb], p3.reshape(tk, nq2).astype(BF16),
                     preferred_element_type=F32)
        acc = acc_scr[...].reshape(HEAD_W // SUBLANES, SUBLANES, nq2) * alpha[None]
        acc_scr[...] = acc.reshape(HEAD_W, nq2) + pv
        m_scr[...] = m_new

    def body(kb, carry):
        block(kb, False)
        return carry

    def body_masked(kb, carry):
        block(kb, True)
        return carry

    n_full = (qi * tq) // tk
    n_all = (qi * tq + tq + tk - 1) // tk
    lax.fori_loop(0, n_full, body, 0)
    lax.fori_loop(n_full, n_all, body_masked, 0)

    ot = acc_scr[...] / l_scr[0:1, :]
    ot = ot[:, :tq] - lam_ref[0, 0] * ot[:, tq:]
    o = ot.T
    o = o * lax.rsqrt(jnp.mean(o * o, axis=-1, keepdims=True) + EPS) * ng_ref[...]
    o_ref[...] = o * out_scale


def _attn_prompt(lam, qt, kdb, vt, norm_g, *, batch, seq, tk, out_scale):
    tq = _pick_tile(seq, ATTN_TQ, LANES)
    nq = seq // tq
    return pl.pallas_call(
        functools.partial(_attn_prompt_body, tq=tq, tk=tk, out_scale=out_scale),
        grid=(batch, N_HEADS, nq),
        in_specs=[pl.BlockSpec(memory_space=pltpu.SMEM),
                  pl.BlockSpec((HEAD_W, tq), lambda b, h, i: (h, b * nq + i)),
                  pl.BlockSpec((seq, HEAD_W), lambda b, h, i: (b, h)),
                  pl.BlockSpec((None, None, seq // tk, HEAD_W, tk),
                               lambda b, h, i: (b, h, 0, 0, 0)),
                  pl.BlockSpec((1, HEAD_W), lambda b, h, i: (0, 0))],
        out_specs=pl.BlockSpec((tq, HEAD_W), lambda b, h, i: (b * nq + i, h)),
        out_shape=jax.ShapeDtypeStruct((batch * seq, GROUP_W), F32),
        scratch_shapes=[pltpu.VMEM((SUBLANES, 2 * tq), F32), pltpu.VMEM((SUBLANES, 2 * tq), F32),
                        pltpu.VMEM((HEAD_W, 2 * tq), F32)],
        compiler_params=pltpu.CompilerParams(
            dimension_semantics=("parallel", "parallel", "arbitrary"),
            vmem_limit_bytes=VMEM_LIMIT),
        name="attn_prompt",
    )(lam, qt, kdb, vt, norm_g)


Q_ROWS = 8
QH_ROWS = 2 * Q_ROWS
QS_ROWS = N_HEADS * QH_ROWS
DECODE_PAGES_PER_STEP = 32


def _attn_decode_body(pt_ref, lam_ref, q_ref, kn_ref, vn_ref, ng_ref, *rest,
                      pages_per_step, n_new, out_scale):
    G = pages_per_step
    k_refs, v_refs = rest[:G], rest[G:2 * G]
    o_ref, m_scr, l_scr, acc_scr = rest[2 * G:]
    j = pl.program_id(1)

    @pl.when(j == 0)
    def _init():
        m_scr[...] = jnp.full(m_scr.shape, -jnp.inf, F32)
        l_scr[...] = jnp.zeros(l_scr.shape, F32)
        acc_scr[...] = jnp.zeros(acc_scr.shape, F32)

    q = q_ref[0]
    lane = lax.broadcasted_iota(I32, (Q_ROWS, HEAD_W), 1)
    zero = jnp.zeros((Q_ROWS, HEAD_W), BF16)
    qh = []
    for h in range(N_HEADS):
        t = q[:, h * HEAD_W:(h + 1) * HEAD_W]
        qh.append(jnp.concatenate([jnp.where(lane < DIFF_HALF, t, zero),
                                   jnp.where(lane >= DIFF_HALF, t, zero)], axis=0))

    def update(ks, vs, mask):
        n = len(ks)
        s = jnp.concatenate(
            [jnp.concatenate([_dot_nt(qh[h], ks[g][h]) for g in range(n)], axis=1)
             for h in range(N_HEADS)], axis=0)
        if mask is not None:
            s = jnp.where(mask, s, -jnp.inf)
        m_prev = m_scr[...]
        m_new = jnp.maximum(m_prev, jnp.max(s, axis=1, keepdims=True))
        alpha = jnp.exp2(m_prev - m_new)
        p = jnp.exp2(s - _rep_lanes(m_new, n)).astype(BF16)
        l_scr[...] = alpha * l_scr[...] + jnp.sum(p.astype(F32), axis=1, keepdims=True)
        pv = []
        for h in range(N_HEADS):
            ph = p[h * QH_ROWS:(h + 1) * QH_ROWS]
            acc = jnp.dot(ph[:, :PAGE], vs[0][h], preferred_element_type=F32)
            for g in range(1, n):
                acc = acc + jnp.dot(ph[:, g * PAGE:(g + 1) * PAGE], vs[g][h],
                                    preferred_element_type=F32)
            pv.append(acc)
        acc_scr[...] = alpha * acc_scr[...] + jnp.concatenate(pv, axis=0)
        m_scr[...] = m_new

    heads = lambda ref: [ref[pl.ds(h, PAGE, stride=N_HEADS), :].astype(BF16)
                         for h in range(N_HEADS)]
    update([heads(k_refs[g]) for g in range(G)], [heads(v_refs[g]) for g in range(G)], None)

    @pl.when(j == pl.num_programs(1) - 1)
    def _fin():
        pad = jnp.zeros((PAGE - Q_ROWS, GROUP_W), F32)
        kn = jnp.concatenate([kn_ref[0], pad], axis=0).astype(BF16)
        vn = jnp.concatenate([vn_ref[0], pad], axis=0).astype(BF16)
        split = lambda a: [a[:, h * HEAD_W:(h + 1) * HEAD_W] for h in range(N_HEADS)]
        t = lax.broadcasted_iota(I32, (QS_ROWS, PAGE), 0) & (Q_ROWS - 1)
        cidx = lax.broadcasted_iota(I32, (QS_ROWS, PAGE), 1)
        update([split(kn)], [split(vn)], (cidx <= t) & (cidx < n_new))
        o = acc_scr[...] / l_scr[...]
        lam = lam_ref[0, 0]
        for h in range(N_HEADS):
            r0 = h * QH_ROWS
            oh = o[r0:r0 + Q_ROWS] - lam * o[r0 + Q_ROWS:r0 + QH_ROWS]
            oh = oh * lax.rsqrt(jnp.mean(oh * oh, axis=-1, keepdims=True) + EPS) * ng_ref[...]
            o_ref[0, :, h * HEAD_W:(h + 1) * HEAD_W] = oh * out_scale


def _attn_decode(page_table, lam, q, kn, vn, norm_g, cache_k, cache_v, *, layer, n_new,
                 out_scale):
    db, n_pages = page_table.shape
    G = DECODE_PAGES_PER_STEP
    while n_pages % G:
        G //= 2
    nj = n_pages // G
    pt_flat = page_table.reshape(-1)

    def page_map(g):
        return lambda b, j, pt: (layer, pt[b * n_pages + j * G + g], 0, 0)

    per_b = lambda a: pl.BlockSpec((1,) + a.shape[1:], lambda b, j, pt: (b, 0, 0))
    page_spec = lambda g: pl.BlockSpec((None, None, PAGE * N_HEADS, HEAD_W), page_map(g))
    as_rows = lambda c: c.reshape(c.shape[0], c.shape[1], PAGE * N_HEADS, HEAD_W)
    grid_spec = pltpu.PrefetchScalarGridSpec(
        num_scalar_prefetch=1,
        grid=(db, nj),
        in_specs=[pl.BlockSpec(memory_space=pltpu.SMEM), per_b(q), per_b(kn), per_b(vn),
                  pl.BlockSpec((1, HEAD_W), lambda b, j, pt: (0, 0))]
                 + [page_spec(g) for g in range(G)] + [page_spec(g) for g in range(G)],
        out_specs=pl.BlockSpec((1, Q_ROWS, GROUP_W), lambda b, j, pt: (b, 0, 0)),
        scratch_shapes=[pltpu.VMEM((QS_ROWS, LANES), F32), pltpu.VMEM((QS_ROWS, LANES), F32),
                        pltpu.VMEM((QS_ROWS, HEAD_W), F32)],
    )
    return pl.pallas_call(
        functools.partial(_attn_decode_body, pages_per_step=G, n_new=n_new,
                          out_scale=out_scale),
        grid_spec=grid_spec,
        out_shape=jax.ShapeDtypeStruct((db, Q_ROWS, GROUP_W), F32),
        compiler_params=pltpu.CompilerParams(
            dimension_semantics=("parallel", "arbitrary"), vmem_limit_bytes=VMEM_LIMIT),
        name="attn_decode",
    )(pt_flat, lam, q, kn, vn, norm_g, *([as_rows(cache_k)] * G), *([as_rows(cache_v)] * G))


def _oproj_router_body(xp_ref, xs_ref, gop_ref, gos_ref, dop_ref, dos_ref, wo_ref, ln_ref,
                       wr_ref, br_ref, h_ref, hn_ref, gate_ref, idx_ref, *, npt):
    is_p = pl.program_id(0) < npt
    x = jnp.where(is_p, xp_ref[...], xs_ref[...])
    go = jnp.where(is_p, gop_ref[...], gos_ref[...])
    do = jnp.where(is_p, dop_ref[...], dos_ref[...])
    h = (x
         + jnp.dot(go.astype(BF16), wo_ref[0:GROUP_W, :], preferred_element_type=F32)
         + jnp.dot(do.astype(BF16), wo_ref[GROUP_W:, :], preferred_element_type=F32))
    h_ref[...] = h
    hn = h * lax.rsqrt(jnp.mean(h * h, axis=-1, keepdims=True) + EPS) * ln_ref[...]
    hn_ref[...] = hn
    logits = _dot3s(_split2s(hn), _split2s(wr_ref[...])) + br_ref[...]
    lane = lax.broadcasted_iota(I32, logits.shape, 1)
    lane_f = lane.astype(F32)
    cur = jnp.where(lane < N_EXPERTS, logits, -jnp.inf)
    vals, idxs = [], []
    for _ in range(TOP_K):
        m = jnp.max(cur, axis=1, keepdims=True)
        i = jnp.min(jnp.where(cur == m, lane_f, float(LANES)), axis=1, keepdims=True)
        vals.append(m)
        idxs.append(i)
        cur = jnp.where(lane_f == i, -jnp.inf, cur)
    es = [jnp.exp(v - vals[0]) for v in vals]
    den = es[0]
    for e in es[1:]:
        den = den + e
    gate = jnp.zeros(logits.shape, F32)
    idx = jnp.zeros(logits.shape, F32)
    for k in range(TOP_K):
        gate = jnp.where(lane == k, es[k] / den, gate)
        idx = jnp.where(lane == k, idxs[k], idx)
    gate_ref[...] = gate
    idx_ref[...] = idx.astype(I32)


def _oproj_router(x_p, x_s, go_p, go_s, do_p, do_s, w_o, ln_g, w_r, b_r):
    n_p, n_s = x_p.shape[0], x_s.shape[0]
    tm = TOKEN_TILE
    npt, nst = n_p // tm, n_s // tm
    n = n_p + n_s
    row = lambda w: pl.BlockSpec((tm, w), lambda i: (i, 0))
    full = lambda a: pl.BlockSpec(a.shape, lambda i: (0,) * a.ndim)
    outs = [(D_MODEL, F32), (D_MODEL, F32), (LANES, F32), (LANES, I32)]
    return pl.pallas_call(
        functools.partial(_oproj_router_body, npt=npt),
        grid=(npt + nst,),
        in_specs=[_prompt_spec(D_MODEL, npt), _sample_spec(D_MODEL, npt),
                  _prompt_spec(GROUP_W, npt), _sample_spec(GROUP_W, npt),
                  _prompt_spec(GROUP_W, npt), _sample_spec(GROUP_W, npt),
                  full(w_o), full(ln_g), full(w_r), full(b_r)],
        out_specs=[row(w) for w, _ in outs],
        out_shape=[jax.ShapeDtypeStruct((n, w), dt) for w, dt in outs],
        compiler_params=pltpu.CompilerParams(
            dimension_semantics=("parallel",), vmem_limit_bytes=VMEM_LIMIT),
        name="oproj_router",
    )(x_p, x_s, go_p, go_s, do_p, do_s, w_o, ln_g, w_r, b_r)


MOE_TILE = 256


def _moe_body(te_ref, nused_ref, xs_ref, wg_ref, wu_ref, wd_ref, bg_ref, bu_ref, bd_ref,
              ys_ref, wg_bf, wu_bf, wd_bf):
    i = pl.program_id(0)
    e = te_ref[i]
    prev = te_ref[jnp.maximum(i - 1, 0)]

    @pl.when((i == 0) | (e != prev))
    def _cast():
        wg_bf[...] = wg_ref[0].astype(BF16)
        wu_bf[...] = wu_ref[0].astype(BF16)
        wd_bf[...] = wd_ref[0].astype(BF16)

    @pl.when(i < nused_ref[0])
    def _compute():
        x = xs_ref[...].astype(BF16)
        gl = jnp.minimum(jnp.dot(x, wg_bf[...], preferred_element_type=F32) + bg_ref[0],
                         SWIGLU_LIMIT)
        lin = jnp.clip(jnp.dot(x, wu_bf[...], preferred_element_type=F32) + bu_ref[0],
                       -SWIGLU_LIMIT, SWIGLU_LIMIT)
        act = gl * _sigmoid(SWIGLU_ALPHA * gl) * (lin + 1.0)
        ys_ref[...] = jnp.dot(act.astype(BF16), wd_bf[...],
                              preferred_element_type=F32) + bd_ref[0]

    @pl.when(i >= nused_ref[0])
    def _idle():
        ys_ref[...] = jnp.zeros(ys_ref.shape, F32)


def _moe_experts(tile_expert, n_used, xs, w_gate, b_gate, w_up, b_up, w_down, b_down):
    r = xs.shape[0]
    tm = MOE_TILE
    d_ff = w_gate.shape[-1]
    wspec = lambda a: pl.BlockSpec((1,) + a.shape[1:], lambda i, te, nu: (te[i], 0, 0))
    grid_spec = pltpu.PrefetchScalarGridSpec(
        num_scalar_prefetch=2,
        grid=(r // tm,),
        in_specs=[pl.BlockSpec((tm, D_MODEL), lambda i, te, nu: (i, 0)),
                  wspec(w_gate), wspec(w_up), wspec(w_down),
                  wspec(b_gate), wspec(b_up), wspec(b_down)],
        out_specs=pl.BlockSpec((tm, D_MODEL), lambda i, te, nu: (i, 0)),
        scratch_shapes=[pltpu.VMEM((D_MODEL, d_ff), BF16), pltpu.VMEM((D_MODEL, d_ff), BF16),
                        pltpu.VMEM((d_ff, D_MODEL), BF16)],
    )
    return pl.pallas_call(
        _moe_body,
        grid_spec=grid_spec,
        out_shape=jax.ShapeDtypeStruct((r, D_MODEL), F32),
        compiler_params=pltpu.CompilerParams(
            dimension_semantics=("arbitrary",), vmem_limit_bytes=VMEM_LIMIT),
        name="moe_experts",
    )(tile_expert, n_used, xs, w_gate, w_up, w_down, b_gate, b_up, b_down)


def _combine_body(h_ref, yg_ref, gate_ref, fg_ref, yp_ref, ys_ref, *, npt):
    gate = gate_ref[...]
    out = h_ref[...]
    for k in range(TOP_K):
        out = out + gate[:, k:k + 1] * yg_ref[k]
    y = out * lax.rsqrt(jnp.mean(out * out, axis=-1, keepdims=True) + EPS) * fg_ref[...]
    i = pl.program_id(0)

    @pl.when(i < npt)
    def _prompt():
        yp_ref[...] = y

    @pl.when(i >= npt)
    def _sample():
        ys_ref[...] = y


def _combine_norm(h, yg, gate, final_g, *, n_p):
    n = h.shape[0]
    tm = TOKEN_TILE
    npt = n_p // tm
    row = lambda w: pl.BlockSpec((tm, w), lambda i: (i, 0))
    return pl.pallas_call(
        functools.partial(_combine_body, npt=npt),
        grid=(n // tm,),
        in_specs=[row(D_MODEL), pl.BlockSpec((TOP_K, tm, D_MODEL), lambda i: (0, i, 0)),
                  row(LANES), pl.BlockSpec((1, D_MODEL), lambda i: (0, 0))],
        out_specs=[_prompt_spec(D_MODEL, npt), _sample_spec(D_MODEL, npt)],
        out_shape=[jax.ShapeDtypeStruct((n_p, D_MODEL), F32),
                   jax.ShapeDtypeStruct((n - n_p, D_MODEL), F32)],
        compiler_params=pltpu.CompilerParams(
            dimension_semantics=("arbitrary",), vmem_limit_bytes=VMEM_LIMIT),
        name="combine_norm",
    )(h, yg, gate, final_g)


def _route(top_i, tile):
    n = top_i.shape[0]
    n_assign = n * TOP_K
    e_flat = top_i.reshape(-1)
    onehot = (e_flat[:, None] == jnp.arange(N_EXPERTS, dtype=I32)[None, :]).astype(I32)
    csum = jnp.cumsum(onehot, axis=0)
    rank = jnp.sum(onehot * csum, axis=1) - 1
    counts = csum[-1]
    padded = ((counts + tile - 1) // tile) * tile
    ends = jnp.cumsum(padded)
    starts = ends - padded
    pos = starts[e_flat] + rank
    n_rows = n_assign + N_EXPERTS * tile
    n_tiles = n_rows // tile
    tile_start = jnp.arange(n_tiles, dtype=I32) * tile
    tile_expert = jnp.minimum(
        jnp.sum((tile_start[:, None] >= ends[None, :]).astype(I32), axis=1), N_EXPERTS - 1)
    n_used = (ends[-1] // tile).astype(I32).reshape(1)
    order = jnp.argsort(e_flat, stable=True).astype(I32)
    sorted_start = jnp.cumsum(counts) - counts
    row = jnp.arange(n_rows, dtype=I32)
    row_e = jnp.repeat(tile_expert, tile)
    j = row - starts[row_e]
    src = order[jnp.clip(sorted_start[row_e] + j, 0, n_assign - 1)]
    row_token = jnp.where(j < counts[row_e], src // TOP_K, 0)
    return pos.astype(I32), row_token.astype(I32), tile_expert.astype(I32), n_used


def kernel(x_prompt, x_sample, cache_k, cache_v, state_gdn, state_conv, page_table, ln1_g, w_in, conv_w, a_log, dt_bias, gdn_norm_g, lambda_q1, lambda_k1, lambda_q2, lambda_k2, diff_norm_g, w_o, ln2_g, w_router, b_router, w_gate, b_gate, w_up, b_up, w_down, b_down, final_g):
    B, L, _ = x_prompt.shape
    DB, T, _ = x_sample.shape
    depth = w_in.shape[0]
    assert depth == 1
    l = 0
    n_p, n_s = B * L, DB * T
    lam_init = 0.8 - 0.6 * math.exp(-0.3 * l)
    s1 = jnp.sum(lambda_q1[l].astype(F32) * lambda_k1[l].astype(F32))
    s2 = jnp.sum(lambda_q2[l].astype(F32) * lambda_k2[l].astype(F32))
    lam = (jnp.exp(s1) - jnp.exp(s2) + lam_init).reshape(1, 1).astype(F32)

    x_p = x_prompt.reshape(n_p, D_MODEL)
    x_s = x_sample.reshape(n_s, D_MODEL)
    wl = w_in[l]
    o_z = CONV_DIM
    o_a = o_z + GROUP_W
    o_q = o_a + 2 * N_HEADS
    w_cat = jnp.concatenate(
        [wl[:, :o_a], wl[:, o_q:], wl[:, o_a:o_q],
         jnp.zeros((D_MODEL, LANES - 2 * N_HEADS), wl.dtype)], axis=1).astype(BF16)
    (qkv, z, ab, qdb, kdb, kd_p, kd_s, vd_p, vd_s, qt, vt) = _inproj(
        x_p, x_s, ln1_g[l].reshape(1, D_MODEL), w_cat, batch=B)

    gpar = jnp.zeros((8, LANES), F32)
    gpar = gpar.at[0, :N_HEADS].set(a_log[l].astype(F32)).at[1, :N_HEADS].set(dt_bias[l].astype(F32))
    norm_g = gdn_norm_g[l].reshape(1, HEAD_W).astype(F32)
    cw = conv_w[l].astype(F32)
    *g_outs, s_p = _gdn_stacked(
        qkv, ab, z, jnp.zeros((B, HIST_ROWS, CONV_DIM), F32),
        jnp.zeros((B, N_HEADS, HEAD_W, HEAD_W), F32), cw, gpar, norm_g,
        n_seq=B, l=L, cin=GDN_CHUNK, valid=GDN_CHUNK, nspec=B, spp=1)
    g_out_p = jnp.concatenate(g_outs, axis=0)

    def pad_tok(a):
        return jnp.pad(a.reshape(DB, T, a.shape[-1]), ((0, 0), (0, Q_ROWS - T), (0, 0)))

    def pad_tok2(a):
        return pad_tok(a).reshape(DB * Q_ROWS, a.shape[-1])

    qkv_s = qkv[n_p:]
    hist_s = jnp.pad(state_conv[l].astype(F32), ((0, 0), (HIST_ROWS - (CONV_W - 1), 0), (0, 0)))
    sps = 4 if DB % 4 == 0 else 1
    g_out_s, s_s = _gdn_stacked(
        pad_tok2(qkv_s), pad_tok2(ab[n_p:]), pad_tok2(z[n_p:]), hist_s,
        state_gdn[l].astype(F32), cw, gpar, norm_g,
        n_seq=DB, l=Q_ROWS, cin=Q_ROWS, valid=T, nspec=1, spp=sps)
    g_out_s = g_out_s.reshape(DB, Q_ROWS, GROUP_W)[:, :T].reshape(n_s, GROUP_W)

    dn_g = diff_norm_g[l].reshape(1, HEAD_W).astype(F32)
    d_out_p = _attn_prompt(lam, qt, kdb, vt, dn_g, batch=B, seq=L, tk=TOKEN_TILE,
                           out_scale=1.0 - lam_init)

    d_out_s = _attn_decode(
        page_table, lam, pad_tok(qdb[n_p:]), pad_tok(kd_s.reshape(n_s, GROUP_W)), pad_tok(vd_s.reshape(n_s, GROUP_W)), dn_g,
        cache_k, cache_v, layer=l, n_new=T, out_scale=1.0 - lam_init)
    d_out_s = d_out_s[:, :T].reshape(n_s, GROUP_W)

    w_r = jnp.pad(w_router[l].astype(F32), ((0, 0), (0, LANES - N_EXPERTS)))
    b_r = jnp.pad(b_router[l].astype(F32), (0, LANES - N_EXPERTS)).reshape(1, LANES)
    h, hn, gate, idx = _oproj_router(x_p, x_s, g_out_p, g_out_s, d_out_p, d_out_s,
                                     w_o[l].astype(BF16), ln2_g[l].reshape(1, D_MODEL), w_r, b_r)

    pos, row_token, tile_expert, n_used = _route(idx[:, :TOP_K], MOE_TILE)
    xs = hn.at[row_token].get(mode="promise_in_bounds")
    ys = _moe_experts(tile_expert, n_used, xs, w_gate[l], b_gate[l][:, None, :],
                      w_up[l], b_up[l][:, None, :], w_down[l], b_down[l][:, None, :])

    yg = ys.at[pos.reshape(n_p + n_s, TOP_K).T].get(mode="promise_in_bounds")
    y_p, y_s = _combine_norm(h, yg, gate, final_g.reshape(1, D_MODEL).astype(F32), n_p=n_p)

    dt_k, dt_v = cache_k.dtype, cache_v.dtype
    conv_p = jnp.stack([qkv[(i + 1) * L - (CONV_W - 1):(i + 1) * L] for i in range(B)])
    conv_s = qkv_s.reshape(DB, T, CONV_DIM)[:, T - (CONV_W - 1):]
    return (y_p.reshape(B, L, D_MODEL),
            y_s.reshape(DB, T, D_MODEL),
            kd_p.reshape(1, B, L, N_HEADS, HEAD_W).astype(dt_k),
            vd_p.reshape(1, B, L, N_HEADS, HEAD_W).astype(dt_v),
            kd_s.reshape(1, DB, T, N_HEADS, HEAD_W).astype(dt_k),
            vd_s.reshape(1, DB, T, N_HEADS, HEAD_W).astype(dt_v),
            s_p[None].astype(state_gdn.dtype),
            conv_p[None].astype(state_conv.dtype),
            s_s[None].astype(state_gdn.dtype),
            conv_s[None].astype(state_conv.dtype))
```

```python
import functools
import math

import jax
import jax.numpy as jnp
from jax import lax
from jax.experimental import pallas as pl
from jax.experimental.pallas import tpu as pltpu

F32 = jnp.float32
BF16 = jnp.bfloat16
I32 = jnp.int32

D_MODEL = 1024
HEAD_W = 128
N_HEADS = 4
GROUP_W = N_HEADS * HEAD_W
CONV_W = 4
CONV_DIM = 3 * GROUP_W
GDN_CHUNK = 64
GDN_STACK_ROWS = 256
DIFF_HALF = HEAD_W // 2
PAGE = 128
N_EXPERTS = 32
TOP_K = 4
SWIGLU_LIMIT = 7.0
SWIGLU_ALPHA = 1.702
EPS = 1e-6
LANES = 128
SUBLANES = 8
LOG2E = 1.4426950408889634
HIST_ROWS = 8
D_IN_PAD = CONV_DIM + 4 * GROUP_W + LANES
VMEM_LIMIT = 56 * 1024 * 1024


def _pick_tile(n, pref, mult=16):
    t = min(pref, n)
    while t > mult and (n % t or t % mult):
        t -= mult
    assert n % t == 0, (n, pref)
    return t


def _dot(a, b):
    return jnp.dot(a.astype(BF16), b.astype(BF16), preferred_element_type=F32)


def _dot_nt(a, b):
    return lax.dot_general(a.astype(BF16), b.astype(BF16), (((1,), (1,)), ((), ())),
                           preferred_element_type=F32)


def _split2s(x):
    hi = x.astype(BF16)
    return hi, (x - hi.astype(F32)).astype(BF16)


def _dot3s(a, b, dot=None):
    dot = dot or (lambda p, q: jnp.dot(p, q, preferred_element_type=F32))
    return dot(a[0], b[0]) + (dot(a[0], b[1]) + dot(a[1], b[0]))


def _sigmoid(x):
    return 1.0 / (1.0 + jnp.exp(-x))


TOKEN_TILE = 512


def _prompt_spec(w, npt):
    return pl.BlockSpec((TOKEN_TILE, w), lambda i: (jnp.minimum(i, npt - 1), 0))


def _sample_spec(w, npt):
    return pl.BlockSpec((TOKEN_TILE, w), lambda i: (jnp.maximum(i - npt, 0), 0))


def _inproj_body(xp_ref, xs_ref, g_ref, w_ref, qkv_ref, z_ref, ab_ref, qdb_ref, kdb_ref,
                 kdp_ref, kds_ref, vdp_ref, vds_ref, qt_ref, vt_ref, *, npt):
    i = pl.program_id(0)
    x = jnp.where(i < npt, xp_ref[...], xs_ref[...])
    ms = jnp.mean(x * x, axis=-1, keepdims=True)
    xn = (x * lax.rsqrt(ms + EPS) * g_ref[...]).astype(BF16)

    def seg(lo, hi):
        return jnp.dot(xn, w_ref[:, lo:hi], preferred_element_type=F32)

    o = CONV_DIM
    qkv_ref[...] = seg(0, o)
    z_ref[...] = seg(o, o + GROUP_W)
    ab_ref[...] = seg(o + 4 * GROUP_W, o + 4 * GROUP_W + LANES)
    qd = seg(o + GROUP_W, o + 2 * GROUP_W) * (DIFF_HALF ** -0.5 * LOG2E)
    qdb_ref[...] = qd.astype(BF16)
    kd = seg(o + 2 * GROUP_W, o + 3 * GROUP_W)
    kdb_ref[...] = kd.astype(BF16)
    vd = seg(o + 3 * GROUP_W, o + 4 * GROUP_W)

    def store_heads(ref, val):
        for h in range(N_HEADS):
            ref[pl.ds(h, TOKEN_TILE, stride=N_HEADS), :] = val[:, h * HEAD_W:(h + 1) * HEAD_W]

    @pl.when(i < npt)
    def _prompt():
        store_heads(kdp_ref, kd)
        store_heads(vdp_ref, vd)
        qt_ref[...] = qd.T.astype(BF16)
        vt_ref[...] = vd.T.astype(BF16).reshape(N_HEADS, HEAD_W, TOKEN_TILE)

    @pl.when(i >= npt)
    def _sample():
        store_heads(kds_ref, kd)
        store_heads(vds_ref, vd)


def _inproj(x_p, x_s, ln_g, w_cat, *, batch):
    n_p, n_s = x_p.shape[0], x_s.shape[0]
    tm = TOKEN_TILE
    assert n_p % (batch * tm) == 0 and n_s % tm == 0
    npt, nst = n_p // tm, n_s // tm
    nkb = npt // batch
    n = n_p + n_s
    row = lambda w: pl.BlockSpec((tm, w), lambda i: (i, 0))
    full = lambda a: pl.BlockSpec(a.shape, lambda i: (0,) * a.ndim)

    def vt_map(i):
        t = jnp.minimum(i, npt - 1)
        return (t // nkb, 0, t % nkb, 0, 0)

    heads_p = pl.BlockSpec((tm * N_HEADS, HEAD_W), lambda i: (jnp.minimum(i, npt - 1), 0))
    heads_s = pl.BlockSpec((tm * N_HEADS, HEAD_W), lambda i: (jnp.maximum(i - npt, 0), 0))

    sds = jax.ShapeDtypeStruct
    return pl.pallas_call(
        functools.partial(_inproj_body, npt=npt),
        grid=(npt + nst,),
        in_specs=[_prompt_spec(D_MODEL, npt), _sample_spec(D_MODEL, npt), full(ln_g), full(w_cat)],
        out_specs=[row(CONV_DIM), row(GROUP_W), row(LANES), row(GROUP_W), row(GROUP_W),
                   heads_p, heads_s, heads_p, heads_s,
                   pl.BlockSpec((GROUP_W, tm), lambda i: (0, jnp.minimum(i, npt - 1))),
                   pl.BlockSpec((None, N_HEADS, None, HEAD_W, tm), vt_map)],
        out_shape=[sds((n, CONV_DIM), F32), sds((n, GROUP_W), F32), sds((n, LANES), F32),
                   sds((n, GROUP_W), BF16), sds((n, GROUP_W), BF16),
                   sds((n_p * N_HEADS, HEAD_W), F32), sds((n_s * N_HEADS, HEAD_W), F32),
                   sds((n_p * N_HEADS, HEAD_W), F32), sds((n_s * N_HEADS, HEAD_W), F32),
                   sds((GROUP_W, n_p), BF16),
                   sds((batch, N_HEADS, nkb, HEAD_W, tm), BF16)],
        compiler_params=pltpu.CompilerParams(
            dimension_semantics=("arbitrary",), vmem_limit_bytes=VMEM_LIMIT),
        name="inproj",
    )(x_p, x_s, ln_g, w_cat)


def _block_unit_lower_inverse(a, eye, blk):
    t = eye - a
    p = a
    n = 1
    while 2 * n < blk:
        if n == 1:
            ps = _split2s(p)
            p = _dot3s(ps, ps)
            t = t + _dot3s(_split2s(t), _split2s(p))
        else:
            p = _dot(p, p)
            t = t + _dot(t, p)
        n *= 2
    return t


def _gdn_stack_body(*refs, nspec, spp, cin, valid, chains_per_stack):
    qkv_refs, ab_refs, z_refs = refs[:nspec], refs[nspec:2 * nspec], refs[2 * nspec:3 * nspec]
    hist0_ref, s0_ref, convw_ref, gpar_ref, ng_ref = refs[3 * nspec:3 * nspec + 5]
    out_refs = refs[3 * nspec + 5:4 * nspec + 5]
    sfin_ref, s_scr, hist_scr = refs[4 * nspec + 5:]
    nseq = nspec * spp
    blk = cin
    c = pl.program_id(1)

    @pl.when(c == 0)
    def _init():
        s_scr[...] = s0_ref[...]
        hist_scr[...] = hist0_ref[...]

    w = convw_ref[...]
    h0 = HIST_ROWS - (CONV_W - 1)
    lane = lax.broadcasted_iota(I32, (cin, LANES), 1)
    rowi = lax.broadcasted_iota(I32, (cin, LANES), 0)
    acts, gbs, zs = [], [], []
    for s in range(nseq):
        ri, off = s // spp, (s % spp) * cin
        x = qkv_refs[ri][off:off + cin, :]
        xp = jnp.concatenate([hist_scr[s], x], axis=0)
        conv = xp[h0:h0 + cin] * w[0:1]
        for j in range(1, CONV_W):
            conv = conv + xp[h0 + j:h0 + j + cin] * w[j:j + 1]
        hist_scr[s] = x[cin - HIST_ROWS:cin]
        acts.append(conv * _sigmoid(conv))
        t = ab_refs[ri][off:off + cin, :]
        sp_in = t + gpar_ref[1:2, :]
        softplus = jnp.maximum(sp_in, 0.0) + jnp.log(1.0 + jnp.exp(-jnp.abs(sp_in)))
        gb = jnp.where(lane < N_HEADS, -jnp.exp(gpar_ref[0:1, :]) * softplus, _sigmoid(t))
        if valid < cin:
            gb = jnp.where(rowi < valid, gb, 0.0)
        gbs.append(gb)
        zs.append(z_refs[ri][off:off + cin, :])

    R = chains_per_stack * blk
    shift = blk.bit_length() - 1
    r = lax.broadcasted_iota(I32, (R, R), 0)
    q_ = lax.broadcasted_iota(I32, (R, R), 1)
    same = (r >> shift) == (q_ >> shift)
    incl = same & (r >= q_)
    strict = same & (r > q_)
    eye = jnp.where(r == q_, 1.0, 0.0).astype(F32)
    tril = jnp.where(incl, 1.0, 0.0).astype(BF16)
    nrep = max(R // LANES, 1)

    def l2n(t):
        return t * lax.rsqrt(jnp.sum(t * t, axis=-1, keepdims=True) + EPS)

    def tn(p, q):
        return lax.dot_general(p, q, (((0,), (0,)), ((), ())), preferred_element_type=F32)

    all_chains = [(s, h) for s in range(nseq) for h in range(N_HEADS)]
    for st in range(len(all_chains) // chains_per_stack):
        chains = all_chains[st * chains_per_stack:(st + 1) * chains_per_stack]

        def stack(fn):
            return jnp.concatenate([fn(s, h) for s, h in chains], axis=0)

        q = stack(lambda s, h: l2n(acts[s][:, h * HEAD_W:(h + 1) * HEAD_W])) * (HEAD_W ** -0.5)
        k = stack(lambda s, h: l2n(acts[s][:, GROUP_W + h * HEAD_W:GROUP_W + (h + 1) * HEAD_W]))
        v = stack(lambda s, h: acts[s][:, 2 * GROUP_W + h * HEAD_W:2 * GROUP_W + (h + 1) * HEAD_W])
        g = stack(lambda s, h: jnp.broadcast_to(gbs[s][:, h:h + 1], (blk, HEAD_W)))
        beta = stack(lambda s, h: jnp.broadcast_to(
            gbs[s][:, N_HEADS + h:N_HEADS + h + 1], (blk, HEAD_W)))
        g_hi = g.astype(BF16)
        g_r1 = g - g_hi.astype(F32)
        g_mid = g_r1.astype(BF16)
        g_lo = (g_r1 - g_mid.astype(F32)).astype(BF16)
        gam = (jnp.dot(tril, g_hi, preferred_element_type=F32)
               + jnp.dot(tril, g_mid, preferred_element_type=F32)
               + jnp.dot(tril, g_lo, preferred_element_type=F32))
        gcol = _rep_lanes(gam, nrep)[:, :R]
        grow = jnp.broadcast_to(gam.T[0:1, :], (R, R))
        dec = jnp.where(incl, jnp.exp(jnp.minimum(gcol - grow, 0.0)), 0.0)
        eg = jnp.exp(gam)
        kb = k.astype(BF16)
        kk = lax.dot_general(kb, kb, (((1,), (1,)), ((), ())), preferred_element_type=F32)
        bcol = _rep_lanes(beta, nrep)[:, :R]
        a_mat = jnp.where(strict, bcol * kk * dec, 0.0)
        t_inv = _block_unit_lower_inverse(a_mat, eye, blk)
        rhs = jnp.concatenate([beta * v, beta * k * eg], axis=1)
        sol = _dot(t_inv, rhs)
        u, w_ = sol[:, :HEAD_W], sol[:, HEAD_W:]
        ws, qs = [], []
        for ci, (s, h) in enumerate(chains):
            rows = slice(ci * blk, (ci + 1) * blk)
            wq = jnp.concatenate([w_[rows], q[rows]], axis=0)
            wq_s = _dot(wq, s_scr[s, h])
            ws.append(wq_s[:blk])
            qs.append(wq_s[blk:])
        delta = u - jnp.concatenate(ws, axis=0)
        qk = lax.dot_general(q.astype(BF16), kb, (((1,), (1,)), ((), ())),
                             preferred_element_type=F32) * dec
        deltab = delta.astype(BF16)
        o = eg * jnp.concatenate(qs, axis=0) + jnp.dot(qk.astype(BF16), deltab,
                                                       preferred_element_type=F32)
        glast = jnp.concatenate(
            [jnp.broadcast_to(gam[(ci + 1) * blk - 1:(ci + 1) * blk, :], (blk, HEAD_W))
             for ci in range(len(chains))], axis=0)
        ktb = (k * jnp.exp(glast - gam)).astype(BF16)
        on = o * lax.rsqrt(jnp.mean(o * o, axis=-1, keepdims=True) + EPS) * ng_ref[...]
        for ci, (s, h) in enumerate(chains):
            rows = slice(ci * blk, (ci + 1) * blk)
            e_last = jnp.exp(jnp.broadcast_to(gam[(ci + 1) * blk - 1:(ci + 1) * blk, :],
                                              (HEAD_W, HEAD_W)))
            s_scr[s, h] = e_last * s_scr[s, h] + tn(ktb[rows], deltab[rows])
            zh = zs[s][:, h * HEAD_W:(h + 1) * HEAD_W]
            ri, off = s // spp, (s % spp) * cin
            out_refs[ri][off:off + cin, h * HEAD_W:(h + 1) * HEAD_W] = on[rows] * (zh * _sigmoid(zh))

    @pl.when(c == pl.num_programs(1) - 1)
    def _fin():
        sfin_ref[...] = s_scr[...]


def _gdn_stacked(qkv, ab, z, hist0, s0, conv_w, gpar, norm_g, *, n_seq, l, cin, valid, nspec, spp):
    per_step = nspec * spp
    assert l % cin == 0 and n_seq % per_step == 0 and (spp == 1 or l == cin)
    assert cin % HIST_ROWS == 0 and cin & (cin - 1) == 0
    nc = l // cin
    ng = n_seq // per_step
    cps = min(GDN_STACK_ROWS // cin, per_step * N_HEADS)
    assert (per_step * N_HEADS) % cps == 0

    def tok(w, i):
        return pl.BlockSpec((spp * cin, w), lambda g, c: ((g * nspec + i) * nc + c, 0))

    per_g = lambda a: pl.BlockSpec((per_step,) + a.shape[1:],
                                   lambda g, c: (g,) + (0,) * (a.ndim - 1))
    full = lambda a: pl.BlockSpec(a.shape, lambda g, c: (0,) * a.ndim)
    toks = lambda w: [tok(w, i) for i in range(nspec)]
    out_tok = pl.BlockSpec((spp * cin, GROUP_W), lambda g, c: (g * nc + c, 0))
    return pl.pallas_call(
        functools.partial(_gdn_stack_body, nspec=nspec, spp=spp, cin=cin, valid=valid,
                          chains_per_stack=cps),
        grid=(ng, nc),
        in_specs=toks(CONV_DIM) + toks(LANES) + toks(GROUP_W)
                 + [per_g(hist0), per_g(s0), full(conv_w), full(gpar), full(norm_g)],
        out_specs=[out_tok] * nspec + [per_g(s0)],
        out_shape=[jax.ShapeDtypeStruct((ng * spp * l, GROUP_W), F32)] * nspec
                  + [jax.ShapeDtypeStruct(s0.shape, F32)],
        scratch_shapes=[pltpu.VMEM((per_step, N_HEADS, HEAD_W, HEAD_W), F32),
                        pltpu.VMEM((per_step, HIST_ROWS, CONV_DIM), F32)],
        compiler_params=pltpu.CompilerParams(
            dimension_semantics=("parallel", "arbitrary"), vmem_limit_bytes=VMEM_LIMIT),
        name="gdn",
    )(*([qkv] * nspec), *([ab] * nspec), *([z] * nspec), hist0, s0, conv_w, gpar, norm_g)


ATTN_TQ = 1024


def _rep_lanes(x, n):
    return x if n == 1 else jnp.concatenate([x] * n, axis=1)


def _sublane_all(x, op):
    for shift in (4, 2, 1):
        x = op(x, pltpu.roll(x, shift, axis=0))
    return x


def _attn_prompt_body(lam_ref, qt_ref, k_ref, vt_ref, ng_ref, o_ref, m_scr, l_scr, acc_scr,
                      *, tq, tk, out_scale):
    qi = pl.program_id(2)
    qt = qt_ref[...]
    row = lax.broadcasted_iota(I32, qt.shape, 0)
    zero = jnp.zeros_like(qt)
    qqt = jnp.concatenate([jnp.where(row < DIFF_HALF, qt, zero),
                           jnp.where(row >= DIFF_HALF, qt, zero)], axis=1)
    m_scr[...] = jnp.full(m_scr.shape, -jnp.inf, F32)
    l_scr[...] = jnp.zeros(l_scr.shape, F32)
    acc_scr[...] = jnp.zeros(acc_scr.shape, F32)
    nq2 = 2 * tq

    def block(kb, masked):
        start = pl.multiple_of(kb * tk, tk)
        k = k_ref[pl.ds(start, tk), :]
        s = jnp.dot(k, qqt, preferred_element_type=F32)
        if masked:
            key = lax.broadcasted_iota(I32, s.shape, 0) + (kb * tk - qi * tq)
            qc = lax.broadcasted_iota(I32, s.shape, 1)
            qc = jnp.where(qc >= tq, qc - tq, qc)
            s = jnp.where(key <= qc, s, -jnp.inf)
        s3 = s.reshape(tk // SUBLANES, SUBLANES, nq2)
        m_prev = m_scr[...]
        m_new = jnp.maximum(m_prev, _sublane_all(jnp.max(s3, axis=0), jnp.maximum))
        alpha = jnp.exp2(m_prev - m_new)
        p3 = jnp.exp2(s3 - m_new[None])
        l_scr[...] = alpha * l_scr[...] + _sublane_all(jnp.sum(p3, axis=0), jnp.add)
        pv = jnp.dot(vt_ref[kb], p3.reshape(tk, nq2).astype(BF16),
                     preferred_element_type=F32)
        acc = acc_scr[...].reshape(HEAD_W // SUBLANES, SUBLANES, nq2) * alpha[None]
        acc_scr[...] = acc.reshape(HEAD_W, nq2) + pv
        m_scr[...] = m_new

    def body(kb, carry):
        block(kb, False)
        return carry

    def body_masked(kb, carry):
        block(kb, True)
        return carry

    n_full = (qi * tq) // tk
    n_all = (qi * tq + tq + tk - 1) // tk
    lax.fori_loop(0, n_full, body, 0)
    lax.fori_loop(n_full, n_all, body_masked, 0)

    ot = acc_scr[...] / l_scr[0:1, :]
    ot = ot[:, :tq] - lam_ref[0, 0] * ot[:, tq:]
    o = ot.T
    o = o * lax.rsqrt(jnp.mean(o * o, axis=-1, keepdims=True) + EPS) * ng_ref[...]
    o_ref[...] = o * out_scale


def _attn_prompt(lam, qt, kdb, vt, norm_g, *, batch, seq, tk, out_scale):
    tq = _pick_tile(seq, ATTN_TQ, LANES)
    nq = seq // tq
    return pl.pallas_call(
        functools.partial(_attn_prompt_body, tq=tq, tk=tk, out_scale=out_scale),
        grid=(batch, N_HEADS, nq),
        in_specs=[pl.BlockSpec(memory_space=pltpu.SMEM),
                  pl.BlockSpec((HEAD_W, tq), lambda b, h, i: (h, b * nq + i)),
                  pl.BlockSpec((seq, HEAD_W), lambda b, h, i: (b, h)),
                  pl.BlockSpec((None, None, seq // tk, HEAD_W, tk),
                               lambda b, h, i: (b, h, 0, 0, 0)),
                  pl.BlockSpec((1, HEAD_W), lambda b, h, i: (0, 0))],
        out_specs=pl.BlockSpec((tq, HEAD_W), lambda b, h, i: (b * nq + i, h)),
        out_shape=jax.ShapeDtypeStruct((batch * seq, GROUP_W), F32),
        scratch_shapes=[pltpu.VMEM((SUBLANES, 2 * tq), F32), pltpu.VMEM((SUBLANES, 2 * tq), F32),
                        pltpu.VMEM((HEAD_W, 2 * tq), F32)],
        compiler_params=pltpu.CompilerParams(
            dimension_semantics=("parallel", "parallel", "arbitrary"),
            vmem_limit_bytes=VMEM_LIMIT),
        name="attn_prompt",
    )(lam, qt, kdb, vt, norm_g)


Q_ROWS = 8
QH_ROWS = 2 * Q_ROWS
QS_ROWS = N_HEADS * QH_ROWS
DECODE_PAGES_PER_STEP = 32


def _attn_decode_body(pt_ref, lam_ref, q_ref, kn_ref, vn_ref, ng_ref, *rest,
                      pages_per_step, n_new, out_scale):
    G = pages_per_step
    k_refs, v_refs = rest[:G], rest[G:2 * G]
    o_ref, m_scr, l_scr, acc_scr = rest[2 * G:]
    j = pl.program_id(1)

    @pl.when(j == 0)
    def _init():
        m_scr[...] = jnp.full(m_scr.shape, -jnp.inf, F32)
        l_scr[...] = jnp.zeros(l_scr.shape, F32)
        acc_scr[...] = jnp.zeros(acc_scr.shape, F32)

    q = q_ref[0]
    lane = lax.broadcasted_iota(I32, (Q_ROWS, HEAD_W), 1)
    zero = jnp.zeros((Q_ROWS, HEAD_W), BF16)
    qh = []
    for h in range(N_HEADS):
        t = q[:, h * HEAD_W:(h + 1) * HEAD_W]
        qh.append(jnp.concatenate([jnp.where(lane < DIFF_HALF, t, zero),
                                   jnp.where(lane >= DIFF_HALF, t, zero)], axis=0))

    def update(ks, vs, mask):
        n = len(ks)
        s = jnp.concatenate(
            [jnp.concatenate([_dot_nt(qh[h], ks[g][h]) for g in range(n)], axis=1)
             for h in range(N_HEADS)], axis=0)
        if mask is not None:
            s = jnp.where(mask, s, -jnp.inf)
        m_prev = m_scr[...]
        m_new = jnp.maximum(m_prev, jnp.max(s, axis=1, keepdims=True))
        alpha = jnp.exp2(m_prev - m_new)
        p = jnp.exp2(s - _rep_lanes(m_new, n)).astype(BF16)
        l_scr[...] = alpha * l_scr[...] + jnp.sum(p.astype(F32), axis=1, keepdims=True)
        pv = []
        for h in range(N_HEADS):
            ph = p[h * QH_ROWS:(h + 1) * QH_ROWS]
            acc = jnp.dot(ph[:, :PAGE], vs[0][h], preferred_element_type=F32)
            for g in range(1, n):
                acc = acc + jnp.dot(ph[:, g * PAGE:(g + 1) * PAGE], vs[g][h],
                                    preferred_element_type=F32)
            pv.append(acc)
        acc_scr[...] = alpha * acc_scr[...] + jnp.concatenate(pv, axis=0)
        m_scr[...] = m_new

    heads = lambda ref: [ref[pl.ds(h, PAGE, stride=N_HEADS), :].astype(BF16)
                         for h in range(N_HEADS)]
    update([heads(k_refs[g]) for g in range(G)], [heads(v_refs[g]) for g in range(G)], None)

    @pl.when(j == pl.num_programs(1) - 1)
    def _fin():
        pad = jnp.zeros((PAGE - Q_ROWS, GROUP_W), F32)
        kn = jnp.concatenate([kn_ref[0], pad], axis=0).astype(BF16)
        vn = jnp.concatenate([vn_ref[0], pad], axis=0).astype(BF16)
        split = lambda a: [a[:, h * HEAD_W:(h + 1) * HEAD_W] for h in range(N_HEADS)]
        t = lax.broadcasted_iota(I32, (QS_ROWS, PAGE), 0) & (Q_ROWS - 1)
        cidx = lax.broadcasted_iota(I32, (QS_ROWS, PAGE), 1)
        update([split(kn)], [split(vn)], (cidx <= t) & (cidx < n_new))
        o = acc_scr[...] / l_scr[...]
        lam = lam_ref[0, 0]
        for h in range(N_HEADS):
            r0 = h * QH_ROWS
            oh = o[r0:r0 + Q_ROWS] - lam * o[r0 + Q_ROWS:r0 + QH_ROWS]
            oh = oh * lax.rsqrt(jnp.mean(oh * oh, axis=-1, keepdims=True) + EPS) * ng_ref[...]
            o_ref[0, :, h * HEAD_W:(h + 1) * HEAD_W] = oh * out_scale


def _attn_decode(page_table, lam, q, kn, vn, norm_g, cache_k, cache_v, *, layer, n_new,
                 out_scale):
    db, n_pages = page_table.shape
    G = DECODE_PAGES_PER_STEP
    while n_pages % G:
        G //= 2
    nj = n_pages // G
    pt_flat = page_table.reshape(-1)

    def page_map(g):
        return lambda b, j, pt: (layer, pt[b * n_pages + j * G + g], 0, 0)

    per_b = lambda a: pl.BlockSpec((1,) + a.shape[1:], lambda b, j, pt: (b, 0, 0))
    page_spec = lambda g: pl.BlockSpec((None, None, PAGE * N_HEADS, HEAD_W), page_map(g))
    as_rows = lambda c: c.reshape(c.shape[0], c.shape[1], PAGE * N_HEADS, HEAD_W)
    grid_spec = pltpu.PrefetchScalarGridSpec(
        num_scalar_prefetch=1,
        grid=(db, nj),
        in_specs=[pl.BlockSpec(memory_space=pltpu.SMEM), per_b(q), per_b(kn), per_b(vn),
                  pl.BlockSpec((1, HEAD_W), lambda b, j, pt: (0, 0))]
                 + [page_spec(g) for g in range(G)] + [page_spec(g) for g in range(G)],
        out_specs=pl.BlockSpec((1, Q_ROWS, GROUP_W), lambda b, j, pt: (b, 0, 0)),
        scratch_shapes=[pltpu.VMEM((QS_ROWS, LANES), F32), pltpu.VMEM((QS_ROWS, LANES), F32),
                        pltpu.VMEM((QS_ROWS, HEAD_W), F32)],
    )
    return pl.pallas_call(
        functools.partial(_attn_decode_body, pages_per_step=G, n_new=n_new,
                          out_scale=out_scale),
        grid_spec=grid_spec,
        out_shape=jax.ShapeDtypeStruct((db, Q_ROWS, GROUP_W), F32),
        compiler_params=pltpu.CompilerParams(
            dimension_semantics=("parallel", "arbitrary"), vmem_limit_bytes=VMEM_LIMIT),
        name="attn_decode",
    )(pt_flat, lam, q, kn, vn, norm_g, *([as_rows(cache_k)] * G), *([as_rows(cache_v)] * G))


def _oproj_router_body(xp_ref, xs_ref, gop_ref, gos_ref, dop_ref, dos_ref, wo_ref, ln_ref,
                       wr_ref, br_ref, h_ref, hn_ref, gate_ref, idx_ref, cnt_ref, cnt_scr,
                       *, npt):
    is_p = pl.program_id(0) < npt
    x = jnp.where(is_p, xp_ref[...], xs_ref[...])
    go = jnp.where(is_p, gop_ref[...], gos_ref[...])
    do = jnp.where(is_p, dop_ref[...], dos_ref[...])
    h = (x
         + jnp.dot(go.astype(BF16), wo_ref[0:GROUP_W, :], preferred_element_type=F32)
         + jnp.dot(do.astype(BF16), wo_ref[GROUP_W:, :], preferred_element_type=F32))
    h_ref[...] = h
    hn = h * lax.rsqrt(jnp.mean(h * h, axis=-1, keepdims=True) + EPS) * ln_ref[...]
    hn_ref[...] = hn
    logits = _dot3s(_split2s(hn), _split2s(wr_ref[...])) + br_ref[...]
    lane = lax.broadcasted_iota(I32, logits.shape, 1)
    lane_f = lane.astype(F32)
    cur = jnp.where(lane < N_EXPERTS, logits, -jnp.inf)
    vals, idxs = [], []
    for _ in range(TOP_K):
        m = jnp.max(cur, axis=1, keepdims=True)
        i = jnp.min(jnp.where(cur == m, lane_f, float(LANES)), axis=1, keepdims=True)
        vals.append(m)
        idxs.append(i)
        cur = jnp.where(lane_f == i, -jnp.inf, cur)
    es = [jnp.exp(v - vals[0]) for v in vals]
    den = es[0]
    for e in es[1:]:
        den = den + e
    @pl.when(pl.program_id(0) == 0)
    def _init():
        cnt_scr[...] = jnp.zeros(cnt_scr.shape, F32)

    onehots = [jnp.where(lane_f == i, 1.0, 0.0) for i in idxs]
    chosen = onehots[0]
    for oh in onehots[1:]:
        chosen = chosen + oh
    tm = logits.shape[0]
    r = lax.broadcasted_iota(I32, (tm, tm), 0)
    c = lax.broadcasted_iota(I32, (tm, tm), 1)
    before = jnp.dot(jnp.where(r > c, 1.0, 0.0).astype(BF16), chosen.astype(BF16),
                     preferred_element_type=F32) + cnt_scr[0:1, :]
    gate = jnp.zeros(logits.shape, F32)
    idx = jnp.zeros(logits.shape, F32)
    for k in range(TOP_K):
        gate = jnp.where(lane == k, es[k] / den, gate)
        idx = jnp.where(lane == k, idxs[k], idx)
        rank = jnp.sum(onehots[k] * before, axis=1, keepdims=True)
        idx = jnp.where(lane == TOP_K + k, rank, idx)
    gate_ref[...] = gate
    idx_ref[...] = idx.astype(I32)
    cnt_scr[...] = cnt_scr[...] + jnp.sum(chosen, axis=0, keepdims=True)
    cnt_ref[...] = cnt_scr[...]


def _oproj_router(x_p, x_s, go_p, go_s, do_p, do_s, w_o, ln_g, w_r, b_r):
    n_p, n_s = x_p.shape[0], x_s.shape[0]
    tm = TOKEN_TILE
    npt, nst = n_p // tm, n_s // tm
    n = n_p + n_s
    row = lambda w: pl.BlockSpec((tm, w), lambda i: (i, 0))
    full = lambda a: pl.BlockSpec(a.shape, lambda i: (0,) * a.ndim)
    outs = [(D_MODEL, F32), (D_MODEL, F32), (LANES, F32), (LANES, I32)]
    cnt_spec = pl.BlockSpec((SUBLANES, LANES), lambda i: (0, 0))
    return pl.pallas_call(
        functools.partial(_oproj_router_body, npt=npt),
        grid=(npt + nst,),
        in_specs=[_prompt_spec(D_MODEL, npt), _sample_spec(D_MODEL, npt),
                  _prompt_spec(GROUP_W, npt), _sample_spec(GROUP_W, npt),
                  _prompt_spec(GROUP_W, npt), _sample_spec(GROUP_W, npt),
                  full(w_o), full(ln_g), full(w_r), full(b_r)],
        out_specs=[row(w) for w, _ in outs] + [cnt_spec],
        out_shape=[jax.ShapeDtypeStruct((n, w), dt) for w, dt in outs]
                  + [jax.ShapeDtypeStruct((SUBLANES, LANES), F32)],
        scratch_shapes=[pltpu.VMEM((SUBLANES, LANES), F32)],
        compiler_params=pltpu.CompilerParams(
            dimension_semantics=("arbitrary",), vmem_limit_bytes=VMEM_LIMIT),
        name="oproj_router",
    )(x_p, x_s, go_p, go_s, do_p, do_s, w_o, ln_g, w_r, b_r)


MOE_TILE = 512


def _moe_body(te_ref, nused_ref, xs_ref, wg_ref, wu_ref, wd_ref, bg_ref, bu_ref, bd_ref,
              ys_ref, wg_bf, wu_bf, wd_bf):
    i = pl.program_id(0)
    e = te_ref[i]
    prev = te_ref[jnp.maximum(i - 1, 0)]

    @pl.when((i == 0) | (e != prev))
    def _cast():
        wg_bf[...] = wg_ref[0].astype(BF16)
        wu_bf[...] = wu_ref[0].astype(BF16)
        wd_bf[...] = wd_ref[0].astype(BF16)

    @pl.when(i < nused_ref[0])
    def _compute():
        x = xs_ref[...].astype(BF16)
        gl = jnp.minimum(jnp.dot(x, wg_bf[...], preferred_element_type=F32) + bg_ref[0],
                         SWIGLU_LIMIT)
        lin = jnp.clip(jnp.dot(x, wu_bf[...], preferred_element_type=F32) + bu_ref[0],
                       -SWIGLU_LIMIT, SWIGLU_LIMIT)
        act = gl * _sigmoid(SWIGLU_ALPHA * gl) * (lin + 1.0)
        ys_ref[...] = jnp.dot(act.astype(BF16), wd_bf[...],
                              preferred_element_type=F32) + bd_ref[0]

    @pl.when(i >= nused_ref[0])
    def _idle():
        ys_ref[...] = jnp.zeros(ys_ref.shape, F32)


def _moe_experts(tile_expert, n_used, xs, w_gate, b_gate, w_up, b_up, w_down, b_down):
    r = xs.shape[0]
    tm = MOE_TILE
    d_ff = w_gate.shape[-1]
    wspec = lambda a: pl.BlockSpec((1,) + a.shape[1:], lambda i, te, nu: (te[i], 0, 0))
    grid_spec = pltpu.PrefetchScalarGridSpec(
        num_scalar_prefetch=2,
        grid=(r // tm,),
        in_specs=[pl.BlockSpec((tm, D_MODEL), lambda i, te, nu: (i, 0)),
                  wspec(w_gate), wspec(w_up), wspec(w_down),
                  wspec(b_gate), wspec(b_up), wspec(b_down)],
        out_specs=pl.BlockSpec((tm, D_MODEL), lambda i, te, nu: (i, 0)),
        scratch_shapes=[pltpu.VMEM((D_MODEL, d_ff), BF16), pltpu.VMEM((D_MODEL, d_ff), BF16),
                        pltpu.VMEM((d_ff, D_MODEL), BF16)],
    )
    return pl.pallas_call(
        _moe_body,
        grid_spec=grid_spec,
        out_shape=jax.ShapeDtypeStruct((r, D_MODEL), F32),
        compiler_params=pltpu.CompilerParams(
            dimension_semantics=("arbitrary",), vmem_limit_bytes=VMEM_LIMIT),
        name="moe_experts",
    )(tile_expert, n_used, xs, w_gate, w_up, w_down, b_gate, b_up, b_down)


def _combine_body(h_ref, yg_ref, gate_ref, fg_ref, yp_ref, ys_ref, *, npt):
    gate = gate_ref[...]
    out = h_ref[...]
    for k in range(TOP_K):
        out = out + gate[:, k:k + 1] * yg_ref[k]
    y = out * lax.rsqrt(jnp.mean(out * out, axis=-1, keepdims=True) + EPS) * fg_ref[...]
    i = pl.program_id(0)

    @pl.when(i < npt)
    def _prompt():
        yp_ref[...] = y

    @pl.when(i >= npt)
    def _sample():
        ys_ref[...] = y


def _combine_norm(h, yg, gate, final_g, *, n_p):
    n = h.shape[0]
    tm = TOKEN_TILE
    npt = n_p // tm
    row = lambda w: pl.BlockSpec((tm, w), lambda i: (i, 0))
    return pl.pallas_call(
        functools.partial(_combine_body, npt=npt),
        grid=(n // tm,),
        in_specs=[row(D_MODEL), pl.BlockSpec((TOP_K, tm, D_MODEL), lambda i: (0, i, 0)),
                  row(LANES), pl.BlockSpec((1, D_MODEL), lambda i: (0, 0))],
        out_specs=[_prompt_spec(D_MODEL, npt), _sample_spec(D_MODEL, npt)],
        out_shape=[jax.ShapeDtypeStruct((n_p, D_MODEL), F32),
                   jax.ShapeDtypeStruct((n - n_p, D_MODEL), F32)],
        compiler_params=pltpu.CompilerParams(
            dimension_semantics=("arbitrary",), vmem_limit_bytes=VMEM_LIMIT),
        name="combine_norm",
    )(h, yg, gate, final_g)


def _route(top_i, rank, counts, tile):
    n_assign = top_i.shape[0] * TOP_K
    e_flat = top_i.reshape(-1)
    padded = ((counts + tile - 1) // tile) * tile
    ends = jnp.cumsum(padded)
    starts = ends - padded
    pos = starts[e_flat] + rank.reshape(-1)
    n_pad = N_EXPERTS * tile
    n_tiles = (n_assign + n_pad) // tile
    tile_start = jnp.arange(n_tiles, dtype=I32) * tile
    tile_expert = jnp.minimum(
        jnp.sum((tile_start[:, None] >= ends[None, :]).astype(I32), axis=1), N_EXPERTS - 1)
    n_used = (ends[-1] // tile).astype(I32).reshape(1)
    fill_ends = jnp.cumsum(padded - counts)
    fill_e = jnp.sum((jnp.arange(n_pad, dtype=I32)[:, None] >= fill_ends[None, :]).astype(I32),
                     axis=1)
    order = jnp.argsort(jnp.concatenate([e_flat, fill_e]), stable=True).astype(I32)
    row_token = jnp.where(order < n_assign, order // TOP_K, 0)
    return pos.astype(I32), row_token.astype(I32), tile_expert.astype(I32), n_used


def kernel(x_prompt, x_sample, cache_k, cache_v, state_gdn, state_conv, page_table, ln1_g, w_in, conv_w, a_log, dt_bias, gdn_norm_g, lambda_q1, lambda_k1, lambda_q2, lambda_k2, diff_norm_g, w_o, ln2_g, w_router, b_router, w_gate, b_gate, w_up, b_up, w_down, b_down, final_g):
    B, L, _ = x_prompt.shape
    DB, T, _ = x_sample.shape
    depth = w_in.shape[0]
    assert depth == 1
    l = 0
    n_p, n_s = B * L, DB * T
    lam_init = 0.8 - 0.6 * math.exp(-0.3 * l)
    s1 = jnp.sum(lambda_q1[l].astype(F32) * lambda_k1[l].astype(F32))
    s2 = jnp.sum(lambda_q2[l].astype(F32) * lambda_k2[l].astype(F32))
    lam = (jnp.exp(s1) - jnp.exp(s2) + lam_init).reshape(1, 1).astype(F32)

    x_p = x_prompt.reshape(n_p, D_MODEL)
    x_s = x_sample.reshape(n_s, D_MODEL)
    wl = w_in[l]
    o_z = CONV_DIM
    o_a = o_z + GROUP_W
    o_q = o_a + 2 * N_HEADS
    w_cat = jnp.concatenate(
        [wl[:, :o_a], wl[:, o_q:], wl[:, o_a:o_q],
         jnp.zeros((D_MODEL, LANES - 2 * N_HEADS), wl.dtype)], axis=1).astype(BF16)
    (qkv, z, ab, qdb, kdb, kd_p, kd_s, vd_p, vd_s, qt, vt) = _inproj(
        x_p, x_s, ln1_g[l].reshape(1, D_MODEL), w_cat, batch=B)

    gpar = jnp.zeros((8, LANES), F32)
    gpar = gpar.at[0, :N_HEADS].set(a_log[l].astype(F32)).at[1, :N_HEADS].set(dt_bias[l].astype(F32))
    norm_g = gdn_norm_g[l].reshape(1, HEAD_W).astype(F32)
    cw = conv_w[l].astype(F32)
    *g_outs, s_p = _gdn_stacked(
        qkv, ab, z, jnp.zeros((B, HIST_ROWS, CONV_DIM), F32),
        jnp.zeros((B, N_HEADS, HEAD_W, HEAD_W), F32), cw, gpar, norm_g,
        n_seq=B, l=L, cin=GDN_CHUNK, valid=GDN_CHUNK, nspec=B, spp=1)
    g_out_p = jnp.concatenate(g_outs, axis=0)

    def pad_tok(a):
        return jnp.pad(a.reshape(DB, T, a.shape[-1]), ((0, 0), (0, Q_ROWS - T), (0, 0)))

    def pad_tok2(a):
        return pad_tok(a).reshape(DB * Q_ROWS, a.shape[-1])

    qkv_s = qkv[n_p:]
    hist_s = jnp.pad(state_conv[l].astype(F32), ((0, 0), (HIST_ROWS - (CONV_W - 1), 0), (0, 0)))
    sps = math.gcd(DB, GDN_STACK_ROWS // (N_HEADS * Q_ROWS))
    g_out_s, s_s = _gdn_stacked(
        pad_tok2(qkv_s), pad_tok2(ab[n_p:]), pad_tok2(z[n_p:]), hist_s,
        state_gdn[l].astype(F32), cw, gpar, norm_g,
        n_seq=DB, l=Q_ROWS, cin=Q_ROWS, valid=T, nspec=1, spp=sps)
    g_out_s = g_out_s.reshape(DB, Q_ROWS, GROUP_W)[:, :T].reshape(n_s, GROUP_W)

    dn_g = diff_norm_g[l].reshape(1, HEAD_W).astype(F32)
    d_out_p = _attn_prompt(lam, qt, kdb, vt, dn_g, batch=B, seq=L, tk=TOKEN_TILE,
                           out_scale=1.0 - lam_init)

    d_out_s = _attn_decode(
        page_table, lam, pad_tok(qdb[n_p:]), pad_tok(kd_s.reshape(n_s, GROUP_W)), pad_tok(vd_s.reshape(n_s, GROUP_W)), dn_g,
        cache_k, cache_v, layer=l, n_new=T, out_scale=1.0 - lam_init)
    d_out_s = d_out_s[:, :T].reshape(n_s, GROUP_W)

    w_r = jnp.pad(w_router[l].astype(F32), ((0, 0), (0, LANES - N_EXPERTS)))
    b_r = jnp.pad(b_router[l].astype(F32), (0, LANES - N_EXPERTS)).reshape(1, LANES)
    h, hn, gate, idx, cnt = _oproj_router(
        x_p, x_s, g_out_p, g_out_s, d_out_p, d_out_s,
        w_o[l].astype(BF16), ln2_g[l].reshape(1, D_MODEL), w_r, b_r)

    pos, row_token, tile_expert, n_used = _route(
        idx[:, :TOP_K], idx[:, TOP_K:2 * TOP_K], cnt[0, :N_EXPERTS].astype(I32), MOE_TILE)
    xs = hn.at[row_token].get(mode="promise_in_bounds")
    ys = _moe_experts(tile_expert, n_used, xs, w_gate[l], b_gate[l][:, None, :],
                      w_up[l], b_up[l][:, None, :], w_down[l], b_down[l][:, None, :])

    yg = ys.at[pos.reshape(n_p + n_s, TOP_K).T].get(mode="promise_in_bounds")
    y_p, y_s = _combine_norm(h, yg, gate, final_g.reshape(1, D_MODEL).astype(F32), n_p=n_p)

    dt_k, dt_v = cache_k.dtype, cache_v.dtype
    conv_p = jnp.stack([qkv[(i + 1) * L - (CONV_W - 1):(i + 1) * L] for i in range(B)])
    conv_s = qkv_s.reshape(DB, T, CONV_DIM)[:, T - (CONV_W - 1):]
    return (y_p.reshape(B, L, D_MODEL),
            y_s.reshape(DB, T, D_MODEL),
            kd_p.reshape(1, B, L, N_HEADS, HEAD_W).astype(dt_k),
            vd_p.reshape(1, B, L, N_HEADS, HEAD_W).astype(dt_v),
            kd_s.reshape(1, DB, T, N_HEADS, HEAD_W).astype(dt_k),
            vd_s.reshape(1, DB, T, N_HEADS, HEAD_W).astype(dt_v),
            s_p[None].astype(state_gdn.dtype),
            conv_p[None].astype(state_conv.dtype),
            s_s[None].astype(state_gdn.dtype),
            conv_s[None].astype(state_conv.dtype))
```

```python
import functools
import math

import jax
import jax.numpy as jnp
from jax import lax
from jax.experimental import pallas as pl
from jax.experimental.pallas import tpu as pltpu

F32 = jnp.float32
BF16 = jnp.bfloat16
I32 = jnp.int32

D_MODEL = 1024
HEAD_W = 128
N_HEADS = 4
GROUP_W = N_HEADS * HEAD_W
CONV_W = 4
CONV_DIM = 3 * GROUP_W
GDN_CHUNK = 64
GDN_STACK_ROWS = 256
DIFF_HALF = HEAD_W // 2
PAGE = 128
N_EXPERTS = 32
TOP_K = 4
SWIGLU_LIMIT = 7.0
SWIGLU_ALPHA = 1.702
EPS = 1e-6
LANES = 128
SUBLANES = 8
LOG2E = 1.4426950408889634
HIST_ROWS = 8
D_IN_PAD = CONV_DIM + 4 * GROUP_W + LANES
VMEM_LIMIT = 56 * 1024 * 1024


def _pick_tile(n, pref, mult=16):
    t = min(pref, n)
    while t > mult and (n % t or t % mult):
        t -= mult
    assert n % t == 0, (n, pref)
    return t


def _dot(a, b):
    return jnp.dot(a.astype(BF16), b.astype(BF16), preferred_element_type=F32)


def _dot_nt(a, b):
    return lax.dot_general(a.astype(BF16), b.astype(BF16), (((1,), (1,)), ((), ())),
                           preferred_element_type=F32)


def _split2s(x):
    hi = x.astype(BF16)
    return hi, (x - hi.astype(F32)).astype(BF16)


def _dot3s(a, b, dot=None):
    dot = dot or (lambda p, q: jnp.dot(p, q, preferred_element_type=F32))
    return dot(a[0], b[0]) + (dot(a[0], b[1]) + dot(a[1], b[0]))


def _sigmoid(x):
    return 1.0 / (1.0 + jnp.exp(-x))


TOKEN_TILE = 512


def _prompt_spec(w, npt):
    return pl.BlockSpec((TOKEN_TILE, w), lambda i: (jnp.minimum(i, npt - 1), 0))


def _sample_spec(w, npt):
    return pl.BlockSpec((TOKEN_TILE, w), lambda i: (jnp.maximum(i - npt, 0), 0))


def _inproj_body(xp_ref, xs_ref, g_ref, w_ref, qkv_ref, z_ref, ab_ref, qdb_ref, kdb_ref,
                 kdp_ref, kds_ref, vdp_ref, vds_ref, qt_ref, vt_ref, *, npt):
    i = pl.program_id(0)
    x = jnp.where(i < npt, xp_ref[...], xs_ref[...])
    ms = jnp.mean(x * x, axis=-1, keepdims=True)
    xn = (x * lax.rsqrt(ms + EPS) * g_ref[...]).astype(BF16)

    def seg(lo, hi):
        return jnp.dot(xn, w_ref[:, lo:hi], preferred_element_type=F32)

    o = CONV_DIM
    qkv_ref[...] = seg(0, o)
    z_ref[...] = seg(o, o + GROUP_W)
    ab_ref[...] = seg(o + 4 * GROUP_W, o + 4 * GROUP_W + LANES)
    qd = seg(o + GROUP_W, o + 2 * GROUP_W) * (DIFF_HALF ** -0.5 * LOG2E)
    qdb_ref[...] = qd.astype(BF16)
    kd = seg(o + 2 * GROUP_W, o + 3 * GROUP_W)
    kdb_ref[...] = kd.astype(BF16)
    vd = seg(o + 3 * GROUP_W, o + 4 * GROUP_W)

    def store_heads(ref, val):
        for h in range(N_HEADS):
            ref[pl.ds(h, TOKEN_TILE, stride=N_HEADS), :] = val[:, h * HEAD_W:(h + 1) * HEAD_W]

    @pl.when(i < npt)
    def _prompt():
        store_heads(kdp_ref, kd)
        store_heads(vdp_ref, vd)
        qt_ref[...] = qd.T.astype(BF16)
        vt_ref[...] = vd.T.astype(BF16).reshape(N_HEADS, HEAD_W, TOKEN_TILE)

    @pl.when(i >= npt)
    def _sample():
        store_heads(kds_ref, kd)
        store_heads(vds_ref, vd)


def _inproj(x_p, x_s, ln_g, w_cat, *, batch):
    n_p, n_s = x_p.shape[0], x_s.shape[0]
    tm = TOKEN_TILE
    assert n_p % (batch * tm) == 0 and n_s % tm == 0
    npt, nst = n_p // tm, n_s // tm
    nkb = npt // batch
    n = n_p + n_s
    row = lambda w: pl.BlockSpec((tm, w), lambda i: (i, 0))
    full = lambda a: pl.BlockSpec(a.shape, lambda i: (0,) * a.ndim)

    def vt_map(i):
        t = jnp.minimum(i, npt - 1)
        return (t // nkb, 0, t % nkb, 0, 0)

    heads_p = pl.BlockSpec((tm * N_HEADS, HEAD_W), lambda i: (jnp.minimum(i, npt - 1), 0))
    heads_s = pl.BlockSpec((tm * N_HEADS, HEAD_W), lambda i: (jnp.maximum(i - npt, 0), 0))

    sds = jax.ShapeDtypeStruct
    return pl.pallas_call(
        functools.partial(_inproj_body, npt=npt),
        grid=(npt + nst,),
        in_specs=[_prompt_spec(D_MODEL, npt), _sample_spec(D_MODEL, npt), full(ln_g), full(w_cat)],
        out_specs=[row(CONV_DIM), row(GROUP_W), row(LANES), row(GROUP_W), row(GROUP_W),
                   heads_p, heads_s, heads_p, heads_s,
                   pl.BlockSpec((GROUP_W, tm), lambda i: (0, jnp.minimum(i, npt - 1))),
                   pl.BlockSpec((None, N_HEADS, None, HEAD_W, tm), vt_map)],
        out_shape=[sds((n, CONV_DIM), F32), sds((n, GROUP_W), F32), sds((n, LANES), F32),
                   sds((n, GROUP_W), BF16), sds((n, GROUP_W), BF16),
                   sds((n_p * N_HEADS, HEAD_W), F32), sds((n_s * N_HEADS, HEAD_W), F32),
                   sds((n_p * N_HEADS, HEAD_W), F32), sds((n_s * N_HEADS, HEAD_W), F32),
                   sds((GROUP_W, n_p), BF16),
                   sds((batch, N_HEADS, nkb, HEAD_W, tm), BF16)],
        compiler_params=pltpu.CompilerParams(
            dimension_semantics=("arbitrary",), vmem_limit_bytes=VMEM_LIMIT),
        name="inproj",
    )(x_p, x_s, ln_g, w_cat)


def _block_unit_lower_inverse(a, eye, blk):
    t = eye - a
    p = a
    n = 1
    while 2 * n < blk:
        if n == 1:
            ps = _split2s(p)
            p = _dot3s(ps, ps)
            t = t + _dot3s(_split2s(t), _split2s(p))
        else:
            p = _dot(p, p)
            t = t + _dot(t, p)
        n *= 2
    return t


def _gdn_stack_body(*refs, nspec, spp, cin, valid, chains_per_stack):
    qkv_refs, ab_refs, z_refs = refs[:nspec], refs[nspec:2 * nspec], refs[2 * nspec:3 * nspec]
    hist0_ref, s0_ref, convw_ref, gpar_ref, ng_ref = refs[3 * nspec:3 * nspec + 5]
    out_refs = refs[3 * nspec + 5:4 * nspec + 5]
    sfin_ref, s_scr, hist_scr = refs[4 * nspec + 5:]
    nseq = nspec * spp
    blk = cin
    c = pl.program_id(1)

    @pl.when(c == 0)
    def _init():
        s_scr[...] = s0_ref[...]
        hist_scr[...] = hist0_ref[...]

    w = convw_ref[...]
    h0 = HIST_ROWS - (CONV_W - 1)
    lane = lax.broadcasted_iota(I32, (cin, LANES), 1)
    rowi = lax.broadcasted_iota(I32, (cin, LANES), 0)
    acts, gbs, zs = [], [], []
    for s in range(nseq):
        ri, off = s // spp, (s % spp) * cin
        x = qkv_refs[ri][off:off + cin, :]
        xp = jnp.concatenate([hist_scr[s], x], axis=0)
        conv = xp[h0:h0 + cin] * w[0:1]
        for j in range(1, CONV_W):
            conv = conv + xp[h0 + j:h0 + j + cin] * w[j:j + 1]
        hist_scr[s] = x[cin - HIST_ROWS:cin]
        acts.append(conv * _sigmoid(conv))
        t = ab_refs[ri][off:off + cin, :]
        sp_in = t + gpar_ref[1:2, :]
        softplus = jnp.maximum(sp_in, 0.0) + jnp.log(1.0 + jnp.exp(-jnp.abs(sp_in)))
        gb = jnp.where(lane < N_HEADS, -jnp.exp(gpar_ref[0:1, :]) * softplus, _sigmoid(t))
        if valid < cin:
            gb = jnp.where(rowi < valid, gb, 0.0)
        gbs.append(gb)
        zs.append(z_refs[ri][off:off + cin, :])

    R = chains_per_stack * blk
    shift = blk.bit_length() - 1
    r = lax.broadcasted_iota(I32, (R, R), 0)
    q_ = lax.broadcasted_iota(I32, (R, R), 1)
    same = (r >> shift) == (q_ >> shift)
    incl = same & (r >= q_)
    strict = same & (r > q_)
    eye = jnp.where(r == q_, 1.0, 0.0).astype(F32)
    tril = jnp.where(incl, 1.0, 0.0).astype(BF16)
    nrep = max(R // LANES, 1)

    def l2n(t):
        return t * lax.rsqrt(jnp.sum(t * t, axis=-1, keepdims=True) + EPS)

    def tn(p, q):
        return lax.dot_general(p, q, (((0,), (0,)), ((), ())), preferred_element_type=F32)

    all_chains = [(s, h) for s in range(nseq) for h in range(N_HEADS)]
    for st in range(len(all_chains) // chains_per_stack):
        chains = all_chains[st * chains_per_stack:(st + 1) * chains_per_stack]

        def stack(fn):
            return jnp.concatenate([fn(s, h) for s, h in chains], axis=0)

        q = stack(lambda s, h: l2n(acts[s][:, h * HEAD_W:(h + 1) * HEAD_W])) * (HEAD_W ** -0.5)
        k = stack(lambda s, h: l2n(acts[s][:, GROUP_W + h * HEAD_W:GROUP_W + (h + 1) * HEAD_W]))
        v = stack(lambda s, h: acts[s][:, 2 * GROUP_W + h * HEAD_W:2 * GROUP_W + (h + 1) * HEAD_W])
        g = stack(lambda s, h: jnp.broadcast_to(gbs[s][:, h:h + 1], (blk, HEAD_W)))
        beta = stack(lambda s, h: jnp.broadcast_to(
            gbs[s][:, N_HEADS + h:N_HEADS + h + 1], (blk, HEAD_W)))
        g_hi = g.astype(BF16)
        g_r1 = g - g_hi.astype(F32)
        g_mid = g_r1.astype(BF16)
        g_lo = (g_r1 - g_mid.astype(F32)).astype(BF16)
        gam = (jnp.dot(tril, g_hi, preferred_element_type=F32)
               + jnp.dot(tril, g_mid, preferred_element_type=F32)
               + jnp.dot(tril, g_lo, preferred_element_type=F32))
        gcol = _rep_lanes(gam, nrep)[:, :R]
        grow = jnp.broadcast_to(gam.T[0:1, :], (R, R))
        dec = jnp.where(incl, jnp.exp(jnp.minimum(gcol - grow, 0.0)), 0.0)
        eg = jnp.exp(gam)
        kb = k.astype(BF16)
        kk = lax.dot_general(kb, kb, (((1,), (1,)), ((), ())), preferred_element_type=F32)
        bcol = _rep_lanes(beta, nrep)[:, :R]
        a_mat = jnp.where(strict, bcol * kk * dec, 0.0)
        t_inv = _block_unit_lower_inverse(a_mat, eye, blk)
        rhs = jnp.concatenate([beta * v, beta * k * eg], axis=1)
        sol = _dot(t_inv, rhs)
        u, w_ = sol[:, :HEAD_W], sol[:, HEAD_W:]
        ws, qs = [], []
        for ci, (s, h) in enumerate(chains):
            rows = slice(ci * blk, (ci + 1) * blk)
            wq = jnp.concatenate([w_[rows], q[rows]], axis=0)
            wq_s = _dot(wq, s_scr[s, h])
            ws.append(wq_s[:blk])
            qs.append(wq_s[blk:])
        delta = u - jnp.concatenate(ws, axis=0)
        qk = lax.dot_general(q.astype(BF16), kb, (((1,), (1,)), ((), ())),
                             preferred_element_type=F32) * dec
        deltab = delta.astype(BF16)
        o = eg * jnp.concatenate(qs, axis=0) + jnp.dot(qk.astype(BF16), deltab,
                                                       preferred_element_type=F32)
        glast = jnp.concatenate(
            [jnp.broadcast_to(gam[(ci + 1) * blk - 1:(ci + 1) * blk, :], (blk, HEAD_W))
             for ci in range(len(chains))], axis=0)
        ktb = (k * jnp.exp(glast - gam)).astype(BF16)
        on = o * lax.rsqrt(jnp.mean(o * o, axis=-1, keepdims=True) + EPS) * ng_ref[...]
        for ci, (s, h) in enumerate(chains):
            rows = slice(ci * blk, (ci + 1) * blk)
            e_last = jnp.exp(jnp.broadcast_to(gam[(ci + 1) * blk - 1:(ci + 1) * blk, :],
                                              (HEAD_W, HEAD_W)))
            s_scr[s, h] = e_last * s_scr[s, h] + tn(ktb[rows], deltab[rows])
            zh = zs[s][:, h * HEAD_W:(h + 1) * HEAD_W]
            ri, off = s // spp, (s % spp) * cin
            out_refs[ri][off:off + cin, h * HEAD_W:(h + 1) * HEAD_W] = on[rows] * (zh * _sigmoid(zh))

    @pl.when(c == pl.num_programs(1) - 1)
    def _fin():
        sfin_ref[...] = s_scr[...]


def _gdn_stacked(qkv, ab, z, hist0, s0, conv_w, gpar, norm_g, *, n_seq, l, cin, valid, nspec, spp):
    per_step = nspec * spp
    assert l % cin == 0 and n_seq % per_step == 0 and (spp == 1 or l == cin)
    assert cin % HIST_ROWS == 0 and cin & (cin - 1) == 0
    nc = l // cin
    ng = n_seq // per_step
    cps = min(GDN_STACK_ROWS // cin, per_step * N_HEADS)
    assert (per_step * N_HEADS) % cps == 0

    def tok(w, i):
        return pl.BlockSpec((spp * cin, w), lambda g, c: ((g * nspec + i) * nc + c, 0))

    per_g = lambda a: pl.BlockSpec((per_step,) + a.shape[1:],
                                   lambda g, c: (g,) + (0,) * (a.ndim - 1))
    full = lambda a: pl.BlockSpec(a.shape, lambda g, c: (0,) * a.ndim)
    toks = lambda w: [tok(w, i) for i in range(nspec)]
    out_tok = pl.BlockSpec((spp * cin, GROUP_W), lambda g, c: (g * nc + c, 0))
    return pl.pallas_call(
        functools.partial(_gdn_stack_body, nspec=nspec, spp=spp, cin=cin, valid=valid,
                          chains_per_stack=cps),
        grid=(ng, nc),
        in_specs=toks(CONV_DIM) + toks(LANES) + toks(GROUP_W)
                 + [per_g(hist0), per_g(s0), full(conv_w), full(gpar), full(norm_g)],
        out_specs=[out_tok] * nspec + [per_g(s0)],
        out_shape=[jax.ShapeDtypeStruct((ng * spp * l, GROUP_W), F32)] * nspec
                  + [jax.ShapeDtypeStruct(s0.shape, F32)],
        scratch_shapes=[pltpu.VMEM((per_step, N_HEADS, HEAD_W, HEAD_W), F32),
                        pltpu.VMEM((per_step, HIST_ROWS, CONV_DIM), F32)],
        compiler_params=pltpu.CompilerParams(
            dimension_semantics=("parallel", "arbitrary"), vmem_limit_bytes=VMEM_LIMIT),
        name="gdn",
    )(*([qkv] * nspec), *([ab] * nspec), *([z] * nspec), hist0, s0, conv_w, gpar, norm_g)


ATTN_TQ = 1024


def _rep_lanes(x, n):
    return x if n == 1 else jnp.concatenate([x] * n, axis=1)


def _sublane_all(x, op):
    for shift in (4, 2, 1):
        x = op(x, pltpu.roll(x, shift, axis=0))
    return x


def _attn_prompt_body(lam_ref, qt_ref, k_ref, vt_ref, ng_ref, o_ref, m_scr, l_scr, acc_scr,
                      *, tq, tk, out_scale):
    qi = pl.program_id(2)
    qt = qt_ref[...]
    row = lax.broadcasted_iota(I32, qt.shape, 0)
    zero = jnp.zeros_like(qt)
    qqt = jnp.concatenate([jnp.where(row < DIFF_HALF, qt, zero),
                           jnp.where(row >= DIFF_HALF, qt, zero)], axis=1)
    m_scr[...] = jnp.full(m_scr.shape, -jnp.inf, F32)
    l_scr[...] = jnp.zeros(l_scr.shape, F32)
    acc_scr[...] = jnp.zeros(acc_scr.shape, F32)
    nq2 = 2 * tq

    def block(kb, masked):
        start = pl.multiple_of(kb * tk, tk)
        k = k_ref[pl.ds(start, tk), :]
        s = jnp.dot(k, qqt, preferred_element_type=F32)
        if masked:
            key = lax.broadcasted_iota(I32, s.shape, 0) + (kb * tk - qi * tq)
            qc = lax.broadcasted_iota(I32, s.shape, 1)
            qc = jnp.where(qc >= tq, qc - tq, qc)
            s = jnp.where(key <= qc, s, -jnp.inf)
        s3 = s.reshape(tk // SUBLANES, SUBLANES, nq2)
        m_prev = m_scr[...]
        m_new = jnp.maximum(m_prev, _sublane_all(jnp.max(s3, axis=0), jnp.maximum))
        alpha = jnp.exp2(m_prev - m_new)
        p3 = jnp.exp2(s3 - m_new[None])
        l_scr[...] = alpha * l_scr[...] + _sublane_all(jnp.sum(p3, axis=0), jnp.add)
        pv = jnp.dot(vt_ref[kb], p3.reshape(tk, nq2).astype(BF16),
                     preferred_element_type=F32)
        acc = acc_scr[...].reshape(HEAD_W // SUBLANES, SUBLANES, nq2) * alpha[None]
        acc_scr[...] = acc.reshape(HEAD_W, nq2) + pv
        m_scr[...] = m_new

    def body(kb, carry):
        block(kb, False)
        return carry

    def body_masked(kb, carry):
        block(kb, True)
        return carry

    n_full = (qi * tq) // tk
    n_all = (qi * tq + tq + tk - 1) // tk
    lax.fori_loop(0, n_full, body, 0)
    lax.fori_loop(n_full, n_all, body_masked, 0)

    ot = acc_scr[...] / l_scr[0:1, :]
    ot = ot[:, :tq] - lam_ref[0, 0] * ot[:, tq:]
    o = ot.T
    o = o * lax.rsqrt(jnp.mean(o * o, axis=-1, keepdims=True) + EPS) * ng_ref[...]
    o_ref[...] = o * out_scale


def _attn_prompt(lam, qt, kdb, vt, norm_g, *, batch, seq, tk, out_scale):
    tq = _pick_tile(seq, ATTN_TQ, LANES)
    nq = seq // tq
    return pl.pallas_call(
        functools.partial(_attn_prompt_body, tq=tq, tk=tk, out_scale=out_scale),
        grid=(batch, N_HEADS, nq),
        in_specs=[pl.BlockSpec(memory_space=pltpu.SMEM),
                  pl.BlockSpec((HEAD_W, tq), lambda b, h, i: (h, b * nq + i)),
                  pl.BlockSpec((seq, HEAD_W), lambda b, h, i: (b, h)),
                  pl.BlockSpec((None, None, seq // tk, HEAD_W, tk),
                               lambda b, h, i: (b, h, 0, 0, 0)),
                  pl.BlockSpec((1, HEAD_W), lambda b, h, i: (0, 0))],
        out_specs=pl.BlockSpec((tq, HEAD_W), lambda b, h, i: (b * nq + i, h)),
        out_shape=jax.ShapeDtypeStruct((batch * seq, GROUP_W), F32),
        scratch_shapes=[pltpu.VMEM((SUBLANES, 2 * tq), F32), pltpu.VMEM((SUBLANES, 2 * tq), F32),
                        pltpu.VMEM((HEAD_W, 2 * tq), F32)],
        compiler_params=pltpu.CompilerParams(
            dimension_semantics=("parallel", "parallel", "arbitrary"),
            vmem_limit_bytes=VMEM_LIMIT),
        name="attn_prompt",
    )(lam, qt, kdb, vt, norm_g)


Q_ROWS = 8
QH_ROWS = 2 * Q_ROWS
QS_ROWS = N_HEADS * QH_ROWS
DECODE_PAGES_PER_STEP = 32


def _attn_decode_body(pt_ref, lam_ref, q_ref, kn_ref, vn_ref, ng_ref, *rest,
                      pages_per_step, n_new, out_scale):
    G = pages_per_step
    k_refs, v_refs = rest[:G], rest[G:2 * G]
    o_ref, m_scr, l_scr, acc_scr = rest[2 * G:]
    j = pl.program_id(1)

    @pl.when(j == 0)
    def _init():
        m_scr[...] = jnp.full(m_scr.shape, -jnp.inf, F32)
        l_scr[...] = jnp.zeros(l_scr.shape, F32)
        acc_scr[...] = jnp.zeros(acc_scr.shape, F32)

    q = q_ref[0]
    lane = lax.broadcasted_iota(I32, (Q_ROWS, HEAD_W), 1)
    zero = jnp.zeros((Q_ROWS, HEAD_W), BF16)
    qh = []
    for h in range(N_HEADS):
        t = q[:, h * HEAD_W:(h + 1) * HEAD_W]
        qh.append(jnp.concatenate([jnp.where(lane < DIFF_HALF, t, zero),
                                   jnp.where(lane >= DIFF_HALF, t, zero)], axis=0))

    def update(ks, vs, mask):
        n = len(ks)
        s = jnp.concatenate(
            [jnp.concatenate([_dot_nt(qh[h], ks[g][h]) for g in range(n)], axis=1)
             for h in range(N_HEADS)], axis=0)
        if mask is not None:
            s = jnp.where(mask, s, -jnp.inf)
        m_prev = m_scr[...]
        m_new = jnp.maximum(m_prev, jnp.max(s, axis=1, keepdims=True))
        alpha = jnp.exp2(m_prev - m_new)
        p = jnp.exp2(s - _rep_lanes(m_new, n)).astype(BF16)
        l_scr[...] = alpha * l_scr[...] + jnp.sum(p.astype(F32), axis=1, keepdims=True)
        pv = []
        for h in range(N_HEADS):
            ph = p[h * QH_ROWS:(h + 1) * QH_ROWS]
            acc = jnp.dot(ph[:, :PAGE], vs[0][h], preferred_element_type=F32)
            for g in range(1, n):
                acc = acc + jnp.dot(ph[:, g * PAGE:(g + 1) * PAGE], vs[g][h],
                                    preferred_element_type=F32)
            pv.append(acc)
        acc_scr[...] = alpha * acc_scr[...] + jnp.concatenate(pv, axis=0)
        m_scr[...] = m_new

    heads = lambda ref: [ref[pl.ds(h, PAGE, stride=N_HEADS), :].astype(BF16)
                         for h in range(N_HEADS)]
    update([heads(k_refs[g]) for g in range(G)], [heads(v_refs[g]) for g in range(G)], None)

    @pl.when(j == pl.num_programs(1) - 1)
    def _fin():
        pad = jnp.zeros((PAGE - Q_ROWS, GROUP_W), F32)
        kn = jnp.concatenate([kn_ref[0], pad], axis=0).astype(BF16)
        vn = jnp.concatenate([vn_ref[0], pad], axis=0).astype(BF16)
        split = lambda a: [a[:, h * HEAD_W:(h + 1) * HEAD_W] for h in range(N_HEADS)]
        t = lax.broadcasted_iota(I32, (QS_ROWS, PAGE), 0) & (Q_ROWS - 1)
        cidx = lax.broadcasted_iota(I32, (QS_ROWS, PAGE), 1)
        update([split(kn)], [split(vn)], (cidx <= t) & (cidx < n_new))
        o = acc_scr[...] / l_scr[...]
        lam = lam_ref[0, 0]
        for h in range(N_HEADS):
            r0 = h * QH_ROWS
            oh = o[r0:r0 + Q_ROWS] - lam * o[r0 + Q_ROWS:r0 + QH_ROWS]
            oh = oh * lax.rsqrt(jnp.mean(oh * oh, axis=-1, keepdims=True) + EPS) * ng_ref[...]
            o_ref[0, :, h * HEAD_W:(h + 1) * HEAD_W] = oh * out_scale


def _attn_decode(page_table, lam, q, kn, vn, norm_g, cache_k, cache_v, *, layer, n_new,
                 out_scale):
    db, n_pages = page_table.shape
    G = DECODE_PAGES_PER_STEP
    while n_pages % G:
        G //= 2
    nj = n_pages // G
    pt_flat = page_table.reshape(-1)

    def page_map(g):
        return lambda b, j, pt: (layer, pt[b * n_pages + j * G + g], 0, 0)

    per_b = lambda a: pl.BlockSpec((1,) + a.shape[1:], lambda b, j, pt: (b, 0, 0))
    page_spec = lambda g: pl.BlockSpec((None, None, PAGE * N_HEADS, HEAD_W), page_map(g))
    as_rows = lambda c: c.reshape(c.shape[0], c.shape[1], PAGE * N_HEADS, HEAD_W)
    grid_spec = pltpu.PrefetchScalarGridSpec(
        num_scalar_prefetch=1,
        grid=(db, nj),
        in_specs=[pl.BlockSpec(memory_space=pltpu.SMEM), per_b(q), per_b(kn), per_b(vn),
                  pl.BlockSpec((1, HEAD_W), lambda b, j, pt: (0, 0))]
                 + [page_spec(g) for g in range(G)] + [page_spec(g) for g in range(G)],
        out_specs=pl.BlockSpec((1, Q_ROWS, GROUP_W), lambda b, j, pt: (b, 0, 0)),
        scratch_shapes=[pltpu.VMEM((QS_ROWS, LANES), F32), pltpu.VMEM((QS_ROWS, LANES), F32),
                        pltpu.VMEM((QS_ROWS, HEAD_W), F32)],
    )
    return pl.pallas_call(
        functools.partial(_attn_decode_body, pages_per_step=G, n_new=n_new,
                          out_scale=out_scale),
        grid_spec=grid_spec,
        out_shape=jax.ShapeDtypeStruct((db, Q_ROWS, GROUP_W), F32),
        compiler_params=pltpu.CompilerParams(
            dimension_semantics=("parallel", "arbitrary"), vmem_limit_bytes=VMEM_LIMIT),
        name="attn_decode",
    )(pt_flat, lam, q, kn, vn, norm_g, *([as_rows(cache_k)] * G), *([as_rows(cache_v)] * G))


def _oproj_router_body(xp_ref, xs_ref, gop_ref, gos_ref, dop_ref, dos_ref, wo_ref, ln_ref,
                       wr_ref, br_ref, h_ref, hn_ref, gate_ref, idx_ref, cnt_ref, cnt_scr,
                       *, npt):
    is_p = pl.program_id(0) < npt
    x = jnp.where(is_p, xp_ref[...], xs_ref[...])
    go = jnp.where(is_p, gop_ref[...], gos_ref[...])
    do = jnp.where(is_p, dop_ref[...], dos_ref[...])
    h = (x
         + jnp.dot(go.astype(BF16), wo_ref[0:GROUP_W, :], preferred_element_type=F32)
         + jnp.dot(do.astype(BF16), wo_ref[GROUP_W:, :], preferred_element_type=F32))
    h_ref[...] = h
    hn = h * lax.rsqrt(jnp.mean(h * h, axis=-1, keepdims=True) + EPS) * ln_ref[...]
    hn_ref[...] = hn
    logits = _dot3s(_split2s(hn), _split2s(wr_ref[...])) + br_ref[...]
    lane = lax.broadcasted_iota(I32, logits.shape, 1)
    lane_f = lane.astype(F32)
    cur = jnp.where(lane < N_EXPERTS, logits, -jnp.inf)
    vals, idxs = [], []
    for _ in range(TOP_K):
        m = jnp.max(cur, axis=1, keepdims=True)
        i = jnp.min(jnp.where(cur == m, lane_f, float(LANES)), axis=1, keepdims=True)
        vals.append(m)
        idxs.append(i)
        cur = jnp.where(lane_f == i, -jnp.inf, cur)
    es = [jnp.exp(v - vals[0]) for v in vals]
    den = es[0]
    for e in es[1:]:
        den = den + e
    @pl.when(pl.program_id(0) == 0)
    def _init():
        cnt_scr[...] = jnp.zeros(cnt_scr.shape, F32)

    onehots = [jnp.where(lane_f == i, 1.0, 0.0) for i in idxs]
    chosen = onehots[0]
    for oh in onehots[1:]:
        chosen = chosen + oh
    tm = logits.shape[0]
    r = lax.broadcasted_iota(I32, (tm, tm), 0)
    c = lax.broadcasted_iota(I32, (tm, tm), 1)
    before = jnp.dot(jnp.where(r > c, 1.0, 0.0).astype(BF16), chosen.astype(BF16),
                     preferred_element_type=F32) + cnt_scr[0:1, :]
    gate = jnp.zeros(logits.shape, F32)
    idx = jnp.zeros(logits.shape, F32)
    for k in range(TOP_K):
        gate = jnp.where(lane == k, es[k] / den, gate)
        idx = jnp.where(lane == k, idxs[k], idx)
        rank = jnp.sum(onehots[k] * before, axis=1, keepdims=True)
        idx = jnp.where(lane == TOP_K + k, rank, idx)
    gate_ref[...] = gate
    idx_ref[...] = idx.astype(I32)
    cnt_scr[...] = cnt_scr[...] + jnp.sum(chosen, axis=0, keepdims=True)
    cnt_ref[...] = cnt_scr[...]


def _oproj_router(x_p, x_s, go_p, go_s, do_p, do_s, w_o, ln_g, w_r, b_r):
    n_p, n_s = x_p.shape[0], x_s.shape[0]
    tm = TOKEN_TILE
    npt, nst = n_p // tm, n_s // tm
    n = n_p + n_s
    row = lambda w: pl.BlockSpec((tm, w), lambda i: (i, 0))
    full = lambda a: pl.BlockSpec(a.shape, lambda i: (0,) * a.ndim)
    outs = [(D_MODEL, F32), (D_MODEL, F32), (LANES, F32), (LANES, I32)]
    cnt_spec = pl.BlockSpec((SUBLANES, LANES), lambda i: (0, 0))
    return pl.pallas_call(
        functools.partial(_oproj_router_body, npt=npt),
        grid=(npt + nst,),
        in_specs=[_prompt_spec(D_MODEL, npt), _sample_spec(D_MODEL, npt),
                  _prompt_spec(GROUP_W, npt), _sample_spec(GROUP_W, npt),
                  _prompt_spec(GROUP_W, npt), _sample_spec(GROUP_W, npt),
                  full(w_o), full(ln_g), full(w_r), full(b_r)],
        out_specs=[row(w) for w, _ in outs] + [cnt_spec],
        out_shape=[jax.ShapeDtypeStruct((n, w), dt) for w, dt in outs]
                  + [jax.ShapeDtypeStruct((SUBLANES, LANES), F32)],
        scratch_shapes=[pltpu.VMEM((SUBLANES, LANES), F32)],
        compiler_params=pltpu.CompilerParams(
            dimension_semantics=("arbitrary",), vmem_limit_bytes=VMEM_LIMIT),
        name="oproj_router",
    )(x_p, x_s, go_p, go_s, do_p, do_s, w_o, ln_g, w_r, b_r)


MOE_TILE = 512


def _moe_body(te_ref, nused_ref, xs_ref, wg_ref, wu_ref, wd_ref, bg_ref, bu_ref, bd_ref,
              ys_ref, wg_bf, wu_bf, wd_bf):
    i = pl.program_id(0)
    e = te_ref[i]
    prev = te_ref[jnp.maximum(i - 1, 0)]

    @pl.when((i == 0) | (e != prev))
    def _cast():
        wg_bf[...] = wg_ref[0].astype(BF16)
        wu_bf[...] = wu_ref[0].astype(BF16)
        wd_bf[...] = wd_ref[0].astype(BF16)

    @pl.when(i < nused_ref[0])
    def _compute():
        x = xs_ref[...].astype(BF16)
        gl = jnp.minimum(jnp.dot(x, wg_bf[...], preferred_element_type=F32) + bg_ref[0],
                         SWIGLU_LIMIT)
        lin = jnp.clip(jnp.dot(x, wu_bf[...], preferred_element_type=F32) + bu_ref[0],
                       -SWIGLU_LIMIT, SWIGLU_LIMIT)
        act = gl * _sigmoid(SWIGLU_ALPHA * gl) * (lin + 1.0)
        ys_ref[...] = jnp.dot(act.astype(BF16), wd_bf[...],
                              preferred_element_type=F32) + bd_ref[0]

    @pl.when(i >= nused_ref[0])
    def _idle():
        ys_ref[...] = jnp.zeros(ys_ref.shape, F32)


def _moe_experts(tile_expert, n_used, xs, w_gate, b_gate, w_up, b_up, w_down, b_down):
    r = xs.shape[0]
    tm = MOE_TILE
    d_ff = w_gate.shape[-1]
    wspec = lambda a: pl.BlockSpec((1,) + a.shape[1:], lambda i, te, nu: (te[i], 0, 0))
    grid_spec = pltpu.PrefetchScalarGridSpec(
        num_scalar_prefetch=2,
        grid=(r // tm,),
        in_specs=[pl.BlockSpec((tm, D_MODEL), lambda i, te, nu: (i, 0)),
                  wspec(w_gate), wspec(w_up), wspec(w_down),
                  wspec(b_gate), wspec(b_up), wspec(b_down)],
        out_specs=pl.BlockSpec((tm, D_MODEL), lambda i, te, nu: (i, 0)),
        scratch_shapes=[pltpu.VMEM((D_MODEL, d_ff), BF16), pltpu.VMEM((D_MODEL, d_ff), BF16),
                        pltpu.VMEM((d_ff, D_MODEL), BF16)],
    )
    return pl.pallas_call(
        _moe_body,
        grid_spec=grid_spec,
        out_shape=jax.ShapeDtypeStruct((r, D_MODEL), F32),
        compiler_params=pltpu.CompilerParams(
            dimension_semantics=("arbitrary",), vmem_limit_bytes=VMEM_LIMIT),
        name="moe_experts",
    )(tile_expert, n_used, xs, w_gate, w_up, w_down, b_gate, b_up, b_down)


def _combine_body(h_ref, yg_ref, gate_ref, fg_ref, yp_ref, ys_ref, *, npt):
    gate = gate_ref[...]
    out = h_ref[...]
    for k in range(TOP_K):
        out = out + gate[:, k:k + 1] * yg_ref[k]
    y = out * lax.rsqrt(jnp.mean(out * out, axis=-1, keepdims=True) + EPS) * fg_ref[...]
    i = pl.program_id(0)

    @pl.when(i < npt)
    def _prompt():
        yp_ref[...] = y

    @pl.when(i >= npt)
    def _sample():
        ys_ref[...] = y


def _combine_norm(h, yg, gate, final_g, *, n_p):
    n = h.shape[0]
    tm = TOKEN_TILE
    npt = n_p // tm
    row = lambda w: pl.BlockSpec((tm, w), lambda i: (i, 0))
    return pl.pallas_call(
        functools.partial(_combine_body, npt=npt),
        grid=(n // tm,),
        in_specs=[row(D_MODEL), pl.BlockSpec((TOP_K, tm, D_MODEL), lambda i: (0, i, 0)),
                  row(LANES), pl.BlockSpec((1, D_MODEL), lambda i: (0, 0))],
        out_specs=[_prompt_spec(D_MODEL, npt), _sample_spec(D_MODEL, npt)],
        out_shape=[jax.ShapeDtypeStruct((n_p, D_MODEL), F32),
                   jax.ShapeDtypeStruct((n - n_p, D_MODEL), F32)],
        compiler_params=pltpu.CompilerParams(
            dimension_semantics=("arbitrary",), vmem_limit_bytes=VMEM_LIMIT),
        name="combine_norm",
    )(h, yg, gate, final_g)


def _route(top_i, rank, counts, tile):
    n_assign = top_i.shape[0] * TOP_K
    e_flat = top_i.reshape(-1)
    padded = ((counts + tile - 1) // tile) * tile
    ends = jnp.cumsum(padded)
    starts = ends - padded
    pos = starts[e_flat] + rank.reshape(-1)
    n_pad = N_EXPERTS * tile
    n_tiles = (n_assign + n_pad) // tile
    tile_start = jnp.arange(n_tiles, dtype=I32) * tile
    tile_expert = jnp.minimum(
        jnp.sum((tile_start[:, None] >= ends[None, :]).astype(I32), axis=1), N_EXPERTS - 1)
    n_used = (ends[-1] // tile).astype(I32).reshape(1)
    fill_ends = jnp.cumsum(padded - counts)
    fill_e = jnp.sum((jnp.arange(n_pad, dtype=I32)[:, None] >= fill_ends[None, :]).astype(I32),
                     axis=1)
    order = jnp.argsort(jnp.concatenate([e_flat, fill_e]), stable=True).astype(I32)
    row_token = jnp.where(order < n_assign, order // TOP_K,
                          (order - n_assign) % top_i.shape[0])
    return pos.astype(I32), row_token.astype(I32), tile_expert.astype(I32), n_used


def kernel(x_prompt, x_sample, cache_k, cache_v, state_gdn, state_conv, page_table, ln1_g, w_in, conv_w, a_log, dt_bias, gdn_norm_g, lambda_q1, lambda_k1, lambda_q2, lambda_k2, diff_norm_g, w_o, ln2_g, w_router, b_router, w_gate, b_gate, w_up, b_up, w_down, b_down, final_g):
    B, L, _ = x_prompt.shape
    DB, T, _ = x_sample.shape
    depth = w_in.shape[0]
    assert depth == 1
    l = 0
    n_p, n_s = B * L, DB * T
    lam_init = 0.8 - 0.6 * math.exp(-0.3 * l)
    s1 = jnp.sum(lambda_q1[l].astype(F32) * lambda_k1[l].astype(F32))
    s2 = jnp.sum(lambda_q2[l].astype(F32) * lambda_k2[l].astype(F32))
    lam = (jnp.exp(s1) - jnp.exp(s2) + lam_init).reshape(1, 1).astype(F32)

    x_p = x_prompt.reshape(n_p, D_MODEL)
    x_s = x_sample.reshape(n_s, D_MODEL)
    wl = w_in[l]
    o_z = CONV_DIM
    o_a = o_z + GROUP_W
    o_q = o_a + 2 * N_HEADS
    w_cat = jnp.concatenate(
        [wl[:, :o_a], wl[:, o_q:], wl[:, o_a:o_q],
         jnp.zeros((D_MODEL, LANES - 2 * N_HEADS), wl.dtype)], axis=1).astype(BF16)
    (qkv, z, ab, qdb, kdb, kd_p, kd_s, vd_p, vd_s, qt, vt) = _inproj(
        x_p, x_s, ln1_g[l].reshape(1, D_MODEL), w_cat, batch=B)

    gpar = jnp.zeros((8, LANES), F32)
    gpar = gpar.at[0, :N_HEADS].set(a_log[l].astype(F32)).at[1, :N_HEADS].set(dt_bias[l].astype(F32))
    norm_g = gdn_norm_g[l].reshape(1, HEAD_W).astype(F32)
    cw = conv_w[l].astype(F32)
    *g_outs, s_p = _gdn_stacked(
        qkv, ab, z, jnp.zeros((B, HIST_ROWS, CONV_DIM), F32),
        jnp.zeros((B, N_HEADS, HEAD_W, HEAD_W), F32), cw, gpar, norm_g,
        n_seq=B, l=L, cin=GDN_CHUNK, valid=GDN_CHUNK, nspec=B, spp=1)
    g_out_p = jnp.concatenate(g_outs, axis=0)

    def pad_tok(a):
        return jnp.pad(a.reshape(DB, T, a.shape[-1]), ((0, 0), (0, Q_ROWS - T), (0, 0)))

    def pad_tok2(a):
        return pad_tok(a).reshape(DB * Q_ROWS, a.shape[-1])

    qkv_s = qkv[n_p:]
    hist_s = jnp.pad(state_conv[l].astype(F32), ((0, 0), (HIST_ROWS - (CONV_W - 1), 0), (0, 0)))
    sps = math.gcd(DB, GDN_STACK_ROWS // (N_HEADS * Q_ROWS))
    g_out_s, s_s = _gdn_stacked(
        pad_tok2(qkv_s), pad_tok2(ab[n_p:]), pad_tok2(z[n_p:]), hist_s,
        state_gdn[l].astype(F32), cw, gpar, norm_g,
        n_seq=DB, l=Q_ROWS, cin=Q_ROWS, valid=T, nspec=1, spp=sps)
    g_out_s = g_out_s.reshape(DB, Q_ROWS, GROUP_W)[:, :T].reshape(n_s, GROUP_W)

    dn_g = diff_norm_g[l].reshape(1, HEAD_W).astype(F32)
    d_out_p = _attn_prompt(lam, qt, kdb, vt, dn_g, batch=B, seq=L, tk=TOKEN_TILE,
                           out_scale=1.0 - lam_init)

    d_out_s = _attn_decode(
        page_table, lam, pad_tok(qdb[n_p:]), pad_tok(kd_s.reshape(n_s, GROUP_W)), pad_tok(vd_s.reshape(n_s, GROUP_W)), dn_g,
        cache_k, cache_v, layer=l, n_new=T, out_scale=1.0 - lam_init)
    d_out_s = d_out_s[:, :T].reshape(n_s, GROUP_W)

    w_r = jnp.pad(w_router[l].astype(F32), ((0, 0), (0, LANES - N_EXPERTS)))
    b_r = jnp.pad(b_router[l].astype(F32), (0, LANES - N_EXPERTS)).reshape(1, LANES)
    h, hn, gate, idx, cnt = _oproj_router(
        x_p, x_s, g_out_p, g_out_s, d_out_p, d_out_s,
        w_o[l].astype(BF16), ln2_g[l].reshape(1, D_MODEL), w_r, b_r)

    pos, row_token, tile_expert, n_used = _route(
        idx[:, :TOP_K], idx[:, TOP_K:2 * TOP_K], cnt[0, :N_EXPERTS].astype(I32), MOE_TILE)
    xs = hn.at[row_token].get(mode="promise_in_bounds")
    ys = _moe_experts(tile_expert, n_used, xs, w_gate[l], b_gate[l][:, None, :],
                      w_up[l], b_up[l][:, None, :], w_down[l], b_down[l][:, None, :])

    yg = ys.at[pos.reshape(n_p + n_s, TOP_K).T].get(mode="promise_in_bounds")
    y_p, y_s = _combine_norm(h, yg, gate, final_g.reshape(1, D_MODEL).astype(F32), n_p=n_p)

    dt_k, dt_v = cache_k.dtype, cache_v.dtype
    conv_p = jnp.stack([qkv[(i + 1) * L - (CONV_W - 1):(i + 1) * L] for i in range(B)])
    conv_s = qkv_s.reshape(DB, T, CONV_DIM)[:, T - (CONV_W - 1):]
    return (y_p.reshape(B, L, D_MODEL),
            y_s.reshape(DB, T, D_MODEL),
            kd_p.reshape(1, B, L, N_HEADS, HEAD_W).astype(dt_k),
            vd_p.reshape(1, B, L, N_HEADS, HEAD_W).astype(dt_v),
            kd_s.reshape(1, DB, T, N_HEADS, HEAD_W).astype(dt_k),
            vd_s.reshape(1, DB, T, N_HEADS, HEAD_W).astype(dt_v),
            s_p[None].astype(state_gdn.dtype),
            conv_p[None].astype(state_conv.dtype),
            s_s[None].astype(state_gdn.dtype),
            conv_s[None].astype(state_conv.dtype))
```

```python
import functools
import math

import jax
import jax.numpy as jnp
from jax import lax
from jax.experimental import pallas as pl
from jax.experimental.pallas import tpu as pltpu

F32 = jnp.float32
BF16 = jnp.bfloat16
I32 = jnp.int32

D_MODEL = 1024
HEAD_W = 128
N_HEADS = 4
GROUP_W = N_HEADS * HEAD_W
CONV_W = 4
CONV_DIM = 3 * GROUP_W
GDN_CHUNK = 64
GDN_STACK_ROWS = 256
DIFF_HALF = HEAD_W // 2
PAGE = 128
N_EXPERTS = 32
TOP_K = 4
SWIGLU_LIMIT = 7.0
SWIGLU_ALPHA = 1.702
EPS = 1e-6
LANES = 128
SUBLANES = 8
LOG2E = 1.4426950408889634
HIST_ROWS = 8
D_IN_PAD = CONV_DIM + 4 * GROUP_W + LANES
VMEM_LIMIT = 56 * 1024 * 1024


def _pick_tile(n, pref, mult=16):
    t = min(pref, n)
    while t > mult and (n % t or t % mult):
        t -= mult
    assert n % t == 0, (n, pref)
    return t


def _dot(a, b):
    return jnp.dot(a.astype(BF16), b.astype(BF16), preferred_element_type=F32)


def _dot_nt(a, b):
    return lax.dot_general(a.astype(BF16), b.astype(BF16), (((1,), (1,)), ((), ())),
                           preferred_element_type=F32)


def _split2s(x):
    hi = x.astype(BF16)
    return hi, (x - hi.astype(F32)).astype(BF16)


def _dot3s(a, b, dot=None):
    dot = dot or (lambda p, q: jnp.dot(p, q, preferred_element_type=F32))
    return dot(a[0], b[0]) + (dot(a[0], b[1]) + dot(a[1], b[0]))


def _sigmoid(x):
    return 1.0 / (1.0 + jnp.exp(-x))


TOKEN_TILE = 512


def _prompt_spec(w, npt):
    return pl.BlockSpec((TOKEN_TILE, w), lambda i: (jnp.minimum(i, npt - 1), 0))


def _sample_spec(w, npt):
    return pl.BlockSpec((TOKEN_TILE, w), lambda i: (jnp.maximum(i - npt, 0), 0))


def _inproj_body(xp_ref, xs_ref, g_ref, w_ref, qkv_ref, z_ref, ab_ref, qdb_ref, kdb_ref,
                 kdp_ref, kds_ref, vdp_ref, vds_ref, qt_ref, vt_ref, *, npt):
    i = pl.program_id(0)
    x = jnp.where(i < npt, xp_ref[...], xs_ref[...])
    ms = jnp.mean(x * x, axis=-1, keepdims=True)
    xn = (x * lax.rsqrt(ms + EPS) * g_ref[...]).astype(BF16)

    def seg(lo, hi):
        return jnp.dot(xn, w_ref[:, lo:hi], preferred_element_type=F32)

    o = CONV_DIM
    qkv_ref[...] = seg(0, o)
    z_ref[...] = seg(o, o + GROUP_W)
    ab_ref[...] = seg(o + 4 * GROUP_W, o + 4 * GROUP_W + LANES)
    qd = seg(o + GROUP_W, o + 2 * GROUP_W) * (DIFF_HALF ** -0.5 * LOG2E)
    qdb_ref[...] = qd.astype(BF16)
    kd = seg(o + 2 * GROUP_W, o + 3 * GROUP_W)
    kdb_ref[...] = kd.astype(BF16)
    vd = seg(o + 3 * GROUP_W, o + 4 * GROUP_W)

    def store_heads(ref, val):
        for h in range(N_HEADS):
            ref[pl.ds(h, TOKEN_TILE, stride=N_HEADS), :] = val[:, h * HEAD_W:(h + 1) * HEAD_W]

    @pl.when(i < npt)
    def _prompt():
        store_heads(kdp_ref, kd)
        store_heads(vdp_ref, vd)
        qt_ref[...] = qd.T.astype(BF16)
        vt_ref[...] = vd.T.astype(BF16).reshape(N_HEADS, HEAD_W, TOKEN_TILE)

    @pl.when(i >= npt)
    def _sample():
        store_heads(kds_ref, kd)
        store_heads(vds_ref, vd)


def _inproj(x_p, x_s, ln_g, w_cat, *, batch):
    n_p, n_s = x_p.shape[0], x_s.shape[0]
    tm = TOKEN_TILE
    assert n_p % (batch * tm) == 0 and n_s % tm == 0
    npt, nst = n_p // tm, n_s // tm
    nkb = npt // batch
    n = n_p + n_s
    row = lambda w: pl.BlockSpec((tm, w), lambda i: (i, 0))
    full = lambda a: pl.BlockSpec(a.shape, lambda i: (0,) * a.ndim)

    def vt_map(i):
        t = jnp.minimum(i, npt - 1)
        return (t // nkb, 0, t % nkb, 0, 0)

    heads_p = pl.BlockSpec((tm * N_HEADS, HEAD_W), lambda i: (jnp.minimum(i, npt - 1), 0))
    heads_s = pl.BlockSpec((tm * N_HEADS, HEAD_W), lambda i: (jnp.maximum(i - npt, 0), 0))

    sds = jax.ShapeDtypeStruct
    return pl.pallas_call(
        functools.partial(_inproj_body, npt=npt),
        grid=(npt + nst,),
        in_specs=[_prompt_spec(D_MODEL, npt), _sample_spec(D_MODEL, npt), full(ln_g), full(w_cat)],
        out_specs=[row(CONV_DIM), row(GROUP_W), row(LANES), row(GROUP_W), row(GROUP_W),
                   heads_p, heads_s, heads_p, heads_s,
                   pl.BlockSpec((GROUP_W, tm), lambda i: (0, jnp.minimum(i, npt - 1))),
                   pl.BlockSpec((None, N_HEADS, None, HEAD_W, tm), vt_map)],
        out_shape=[sds((n, CONV_DIM), F32), sds((n, GROUP_W), F32), sds((n, LANES), F32),
                   sds((n, GROUP_W), BF16), sds((n, GROUP_W), BF16),
                   sds((n_p * N_HEADS, HEAD_W), F32), sds((n_s * N_HEADS, HEAD_W), F32),
                   sds((n_p * N_HEADS, HEAD_W), F32), sds((n_s * N_HEADS, HEAD_W), F32),
                   sds((GROUP_W, n_p), BF16),
                   sds((batch, N_HEADS, nkb, HEAD_W, tm), BF16)],
        compiler_params=pltpu.CompilerParams(
            dimension_semantics=("arbitrary",), vmem_limit_bytes=VMEM_LIMIT),
        name="inproj",
    )(x_p, x_s, ln_g, w_cat)


def _block_unit_lower_inverse(a, eye, blk):
    t = eye - a
    p = a
    n = 1
    while 2 * n < blk:
        if n == 1:
            ps = _split2s(p)
            p = _dot3s(ps, ps)
            t = t + _dot3s(_split2s(t), _split2s(p))
        else:
            p = _dot(p, p)
            t = t + _dot(t, p)
        n *= 2
    return t


def _gdn_stack_body(*refs, nspec, spp, cin, valid, chains_per_stack):
    qkv_refs, ab_refs, z_refs = refs[:nspec], refs[nspec:2 * nspec], refs[2 * nspec:3 * nspec]
    hist0_ref, s0_ref, convw_ref, gpar_ref, ng_ref = refs[3 * nspec:3 * nspec + 5]
    out_refs = refs[3 * nspec + 5:4 * nspec + 5]
    sfin_ref, s_scr, hist_scr = refs[4 * nspec + 5:]
    nseq = nspec * spp
    blk = cin
    c = pl.program_id(1)

    @pl.when(c == 0)
    def _init():
        s_scr[...] = s0_ref[...]
        hist_scr[...] = hist0_ref[...]

    w = convw_ref[...]
    h0 = HIST_ROWS - (CONV_W - 1)
    lane = lax.broadcasted_iota(I32, (cin, LANES), 1)
    rowi = lax.broadcasted_iota(I32, (cin, LANES), 0)
    acts, gbs, zs = [], [], []
    for s in range(nseq):
        ri, off = s // spp, (s % spp) * cin
        x = qkv_refs[ri][off:off + cin, :]
        xp = jnp.concatenate([hist_scr[s], x], axis=0)
        conv = xp[h0:h0 + cin] * w[0:1]
        for j in range(1, CONV_W):
            conv = conv + xp[h0 + j:h0 + j + cin] * w[j:j + 1]
        hist_scr[s] = x[cin - HIST_ROWS:cin]
        acts.append(conv * _sigmoid(conv))
        t = ab_refs[ri][off:off + cin, :]
        sp_in = t + gpar_ref[1:2, :]
        softplus = jnp.maximum(sp_in, 0.0) + jnp.log(1.0 + jnp.exp(-jnp.abs(sp_in)))
        gb = jnp.where(lane < N_HEADS, -jnp.exp(gpar_ref[0:1, :]) * softplus, _sigmoid(t))
        if valid < cin:
            gb = jnp.where(rowi < valid, gb, 0.0)
        gbs.append(gb)
        zs.append(z_refs[ri][off:off + cin, :])

    R = chains_per_stack * blk
    shift = blk.bit_length() - 1
    r = lax.broadcasted_iota(I32, (R, R), 0)
    q_ = lax.broadcasted_iota(I32, (R, R), 1)
    same = (r >> shift) == (q_ >> shift)
    incl = same & (r >= q_)
    strict = same & (r > q_)
    eye = jnp.where(r == q_, 1.0, 0.0).astype(F32)
    tril = jnp.where(incl, 1.0, 0.0).astype(BF16)
    nrep = max(R // LANES, 1)

    def l2n(t):
        return t * lax.rsqrt(jnp.sum(t * t, axis=-1, keepdims=True) + EPS)

    def tn(p, q):
        return lax.dot_general(p, q, (((0,), (0,)), ((), ())), preferred_element_type=F32)

    all_chains = [(s, h) for s in range(nseq) for h in range(N_HEADS)]
    for st in range(len(all_chains) // chains_per_stack):
        chains = all_chains[st * chains_per_stack:(st + 1) * chains_per_stack]

        def stack(fn):
            return jnp.concatenate([fn(s, h) for s, h in chains], axis=0)

        q = stack(lambda s, h: l2n(acts[s][:, h * HEAD_W:(h + 1) * HEAD_W])) * (HEAD_W ** -0.5)
        k = stack(lambda s, h: l2n(acts[s][:, GROUP_W + h * HEAD_W:GROUP_W + (h + 1) * HEAD_W]))
        v = stack(lambda s, h: acts[s][:, 2 * GROUP_W + h * HEAD_W:2 * GROUP_W + (h + 1) * HEAD_W])
        g = stack(lambda s, h: jnp.broadcast_to(gbs[s][:, h:h + 1], (blk, HEAD_W)))
        beta = stack(lambda s, h: jnp.broadcast_to(
            gbs[s][:, N_HEADS + h:N_HEADS + h + 1], (blk, HEAD_W)))
        g_hi = g.astype(BF16)
        g_r1 = g - g_hi.astype(F32)
        g_mid = g_r1.astype(BF16)
        g_lo = (g_r1 - g_mid.astype(F32)).astype(BF16)
        gam = (jnp.dot(tril, g_hi, preferred_element_type=F32)
               + jnp.dot(tril, g_mid, preferred_element_type=F32)
               + jnp.dot(tril, g_lo, preferred_element_type=F32))
        gcol = _rep_lanes(gam, nrep)[:, :R]
        grow = jnp.broadcast_to(gam.T[0:1, :], (R, R))
        dec = jnp.where(incl, jnp.exp(jnp.minimum(gcol - grow, 0.0)), 0.0)
        eg = jnp.exp(gam)
        kb = k.astype(BF16)
        kk = lax.dot_general(kb, kb, (((1,), (1,)), ((), ())), preferred_element_type=F32)
        bcol = _rep_lanes(beta, nrep)[:, :R]
        a_mat = jnp.where(strict, bcol * kk * dec, 0.0)
        t_inv = _block_unit_lower_inverse(a_mat, eye, blk)
        rhs = jnp.concatenate([beta * v, beta * k * eg], axis=1)
        sol = _dot(t_inv, rhs)
        u, w_ = sol[:, :HEAD_W], sol[:, HEAD_W:]
        ws, qs = [], []
        for ci, (s, h) in enumerate(chains):
            rows = slice(ci * blk, (ci + 1) * blk)
            wq = jnp.concatenate([w_[rows], q[rows]], axis=0)
            wq_s = _dot(wq, s_scr[s, h])
            ws.append(wq_s[:blk])
            qs.append(wq_s[blk:])
        delta = u - jnp.concatenate(ws, axis=0)
        qk = lax.dot_general(q.astype(BF16), kb, (((1,), (1,)), ((), ())),
                             preferred_element_type=F32) * dec
        deltab = delta.astype(BF16)
        o = eg * jnp.concatenate(qs, axis=0) + jnp.dot(qk.astype(BF16), deltab,
                                                       preferred_element_type=F32)
        glast = jnp.concatenate(
            [jnp.broadcast_to(gam[(ci + 1) * blk - 1:(ci + 1) * blk, :], (blk, HEAD_W))
             for ci in range(len(chains))], axis=0)
        ktb = (k * jnp.exp(glast - gam)).astype(BF16)
        on = o * lax.rsqrt(jnp.mean(o * o, axis=-1, keepdims=True) + EPS) * ng_ref[...]
        for ci, (s, h) in enumerate(chains):
            rows = slice(ci * blk, (ci + 1) * blk)
            e_last = jnp.exp(jnp.broadcast_to(gam[(ci + 1) * blk - 1:(ci + 1) * blk, :],
                                              (HEAD_W, HEAD_W)))
            s_scr[s, h] = e_last * s_scr[s, h] + tn(ktb[rows], deltab[rows])
            zh = zs[s][:, h * HEAD_W:(h + 1) * HEAD_W]
            ri, off = s // spp, (s % spp) * cin
            out_refs[ri][off:off + cin, h * HEAD_W:(h + 1) * HEAD_W] = on[rows] * (zh * _sigmoid(zh))

    @pl.when(c == pl.num_programs(1) - 1)
    def _fin():
        sfin_ref[...] = s_scr[...]


def _gdn_stacked(qkv, ab, z, hist0, s0, conv_w, gpar, norm_g, *, n_seq, l, cin, valid, nspec, spp):
    per_step = nspec * spp
    assert l % cin == 0 and n_seq % per_step == 0 and (spp == 1 or l == cin)
    assert cin % HIST_ROWS == 0 and cin & (cin - 1) == 0
    nc = l // cin
    ng = n_seq // per_step
    cps = min(GDN_STACK_ROWS // cin, per_step * N_HEADS)
    assert (per_step * N_HEADS) % cps == 0

    def tok(w, i):
        return pl.BlockSpec((spp * cin, w), lambda g, c: ((g * nspec + i) * nc + c, 0))

    per_g = lambda a: pl.BlockSpec((per_step,) + a.shape[1:],
                                   lambda g, c: (g,) + (0,) * (a.ndim - 1))
    full = lambda a: pl.BlockSpec(a.shape, lambda g, c: (0,) * a.ndim)
    toks = lambda w: [tok(w, i) for i in range(nspec)]
    out_tok = pl.BlockSpec((spp * cin, GROUP_W), lambda g, c: (g * nc + c, 0))
    return pl.pallas_call(
        functools.partial(_gdn_stack_body, nspec=nspec, spp=spp, cin=cin, valid=valid,
                          chains_per_stack=cps),
        grid=(ng, nc),
        in_specs=toks(CONV_DIM) + toks(LANES) + toks(GROUP_W)
                 + [per_g(hist0), per_g(s0), full(conv_w), full(gpar), full(norm_g)],
        out_specs=[out_tok] * nspec + [per_g(s0)],
        out_shape=[jax.ShapeDtypeStruct((ng * spp * l, GROUP_W), F32)] * nspec
                  + [jax.ShapeDtypeStruct(s0.shape, F32)],
        scratch_shapes=[pltpu.VMEM((per_step, N_HEADS, HEAD_W, HEAD_W), F32),
                        pltpu.VMEM((per_step, HIST_ROWS, CONV_DIM), F32)],
        compiler_params=pltpu.CompilerParams(
            dimension_semantics=("parallel", "arbitrary"), vmem_limit_bytes=VMEM_LIMIT),
        name="gdn",
    )(*([qkv] * nspec), *([ab] * nspec), *([z] * nspec), hist0, s0, conv_w, gpar, norm_g)


ATTN_TQ = 1024


def _rep_lanes(x, n):
    return x if n == 1 else jnp.concatenate([x] * n, axis=1)


def _sublane_all(x, op):
    for shift in (4, 2, 1):
        x = op(x, pltpu.roll(x, shift, axis=0))
    return x


def _attn_prompt_body(lam_ref, qt_ref, k_ref, vt_ref, ng_ref, o_ref, m_scr, l_scr, acc_scr,
                      *, tq, tk, out_scale):
    qi = pl.program_id(2)
    qt = qt_ref[...]
    row = lax.broadcasted_iota(I32, qt.shape, 0)
    zero = jnp.zeros_like(qt)
    qqt = jnp.concatenate([jnp.where(row < DIFF_HALF, qt, zero),
                           jnp.where(row >= DIFF_HALF, qt, zero)], axis=1)
    m_scr[...] = jnp.full(m_scr.shape, -jnp.inf, F32)
    l_scr[...] = jnp.zeros(l_scr.shape, F32)
    acc_scr[...] = jnp.zeros(acc_scr.shape, F32)
    nq2 = 2 * tq

    def block(kb, masked):
        start = pl.multiple_of(kb * tk, tk)
        k = k_ref[pl.ds(start, tk), :]
        s = jnp.dot(k, qqt, preferred_element_type=F32)
        if masked:
            key = lax.broadcasted_iota(I32, s.shape, 0) + (kb * tk - qi * tq)
            qc = lax.broadcasted_iota(I32, s.shape, 1)
            qc = jnp.where(qc >= tq, qc - tq, qc)
            s = jnp.where(key <= qc, s, -jnp.inf)
        s3 = s.reshape(tk // SUBLANES, SUBLANES, nq2)
        m_prev = m_scr[...]
        m_new = jnp.maximum(m_prev, _sublane_all(jnp.max(s3, axis=0), jnp.maximum))
        alpha = jnp.exp2(m_prev - m_new)
        p3 = jnp.exp2(s3 - m_new[None])
        l_scr[...] = alpha * l_scr[...] + _sublane_all(jnp.sum(p3, axis=0), jnp.add)
        pv = jnp.dot(vt_ref[kb], p3.reshape(tk, nq2).astype(BF16),
                     preferred_element_type=F32)
        acc = acc_scr[...].reshape(HEAD_W // SUBLANES, SUBLANES, nq2) * alpha[None]
        acc_scr[...] = acc.reshape(HEAD_W, nq2) + pv
        m_scr[...] = m_new

    def body(kb, carry):
        block(kb, False)
        return carry

    def body_masked(kb, carry):
        block(kb, True)
        return carry

    n_full = (qi * tq) // tk
    n_all = (qi * tq + tq + tk - 1) // tk
    lax.fori_loop(0, n_full, body, 0)
    lax.fori_loop(n_full, n_all, body_masked, 0)

    ot = acc_scr[...] / l_scr[0:1, :]
    ot = ot[:, :tq] - lam_ref[0, 0] * ot[:, tq:]
    o = ot.T
    o = o * lax.rsqrt(jnp.mean(o * o, axis=-1, keepdims=True) + EPS) * ng_ref[...]
    o_ref[...] = o * out_scale


def _attn_prompt(lam, qt, kdb, vt, norm_g, *, batch, seq, tk, out_scale):
    tq = _pick_tile(seq, ATTN_TQ, LANES)
    nq = seq // tq
    return pl.pallas_call(
        functools.partial(_attn_prompt_body, tq=tq, tk=tk, out_scale=out_scale),
        grid=(batch, N_HEADS, nq),
        in_specs=[pl.BlockSpec(memory_space=pltpu.SMEM),
                  pl.BlockSpec((HEAD_W, tq), lambda b, h, i: (h, b * nq + i)),
                  pl.BlockSpec((seq, HEAD_W), lambda b, h, i: (b, h)),
                  pl.BlockSpec((None, None, seq // tk, HEAD_W, tk),
                               lambda b, h, i: (b, h, 0, 0, 0)),
                  pl.BlockSpec((1, HEAD_W), lambda b, h, i: (0, 0))],
        out_specs=pl.BlockSpec((tq, HEAD_W), lambda b, h, i: (b * nq + i, h)),
        out_shape=jax.ShapeDtypeStruct((batch * seq, GROUP_W), F32),
        scratch_shapes=[pltpu.VMEM((SUBLANES, 2 * tq), F32), pltpu.VMEM((SUBLANES, 2 * tq), F32),
                        pltpu.VMEM((HEAD_W, 2 * tq), F32)],
        compiler_params=pltpu.CompilerParams(
            dimension_semantics=("parallel", "parallel", "arbitrary"),
            vmem_limit_bytes=VMEM_LIMIT),
        name="attn_prompt",
    )(lam, qt, kdb, vt, norm_g)


Q_ROWS = 8
QH_ROWS = 2 * Q_ROWS
QS_ROWS = N_HEADS * QH_ROWS
DECODE_PAGES_PER_STEP = 32
PAGE_ROWS = PAGE * N_HEADS


def _attn_decode_body(pt_ref, lam_ref, q_ref, kn_ref, vn_ref, ng_ref, *rest,
                      pages_per_step, n_new, out_scale):
    G = pages_per_step
    k_refs, v_refs = rest[:G], rest[G:2 * G]
    o_ref, m_scr, l_scr, acc_scr = rest[2 * G:]
    j = pl.program_id(1)

    @pl.when(j == 0)
    def _init():
        m_scr[...] = jnp.full(m_scr.shape, -jnp.inf, F32)
        l_scr[...] = jnp.zeros(l_scr.shape, F32)
        acc_scr[...] = jnp.zeros(acc_scr.shape, F32)

    q = q_ref[0]
    lane = lax.broadcasted_iota(I32, (Q_ROWS, HEAD_W), 1)
    zero = jnp.zeros((Q_ROWS, HEAD_W), BF16)
    qs = []
    for h in range(N_HEADS):
        t = q[:, h * HEAD_W:(h + 1) * HEAD_W]
        qs += [jnp.where(lane < DIFF_HALF, t, zero), jnp.where(lane >= DIFF_HALF, t, zero)]
    qs = jnp.concatenate(qs, axis=0)

    def update(ks, vs, masks):
        s = jnp.concatenate([jnp.where(mk, _dot_nt(qs, k), -jnp.inf)
                             for k, mk in zip(ks, masks)], axis=1)
        m_prev = m_scr[...]
        m_new = jnp.maximum(m_prev, jnp.max(s, axis=1, keepdims=True))
        alpha = jnp.exp2(m_prev - m_new)
        p = jnp.exp2(s - _rep_lanes(m_new, s.shape[1] // LANES))
        l_scr[...] = alpha * l_scr[...] + jnp.sum(p, axis=1, keepdims=True)
        p = p.astype(BF16)
        pv, c0 = None, 0
        for v in vs:
            d = jnp.dot(p[:, c0:c0 + v.shape[0]], v, preferred_element_type=F32)
            pv = d if pv is None else pv + d
            c0 += v.shape[0]
        acc_scr[...] = alpha * acc_scr[...] + pv
        m_scr[...] = m_new

    def head_masks(rows):
        r = lax.broadcasted_iota(I32, (QS_ROWS, rows), 0)
        c = lax.broadcasted_iota(I32, (QS_ROWS, rows), 1)
        return r, c, (c & (N_HEADS - 1)) == (r >> (QH_ROWS.bit_length() - 1))

    same_head = head_masks(PAGE_ROWS)[2]
    update([k_refs[g][...].astype(BF16) for g in range(G)],
           [v_refs[g][...].astype(BF16) for g in range(G)], [same_head] * G)

    @pl.when(j == pl.num_programs(1) - 1)
    def _fin():
        pad = jnp.zeros((LANES - kn_ref.shape[1], HEAD_W), F32)
        kn = jnp.concatenate([kn_ref[0], pad], axis=0).astype(BF16)
        vn = jnp.concatenate([vn_ref[0], pad], axis=0).astype(BF16)
        r, c, same = head_masks(LANES)
        tok = c >> (N_HEADS.bit_length() - 1)
        update([kn], [vn], [same & (tok <= (r & (Q_ROWS - 1))) & (tok < n_new)])
        o = acc_scr[...] / l_scr[...]
        lam = lam_ref[0, 0]
        for h in range(N_HEADS):
            r0 = h * QH_ROWS
            oh = o[r0:r0 + Q_ROWS] - lam * o[r0 + Q_ROWS:r0 + QH_ROWS]
            oh = oh * lax.rsqrt(jnp.mean(oh * oh, axis=-1, keepdims=True) + EPS) * ng_ref[...]
            o_ref[0, :, h * HEAD_W:(h + 1) * HEAD_W] = oh * out_scale


def _attn_decode(page_table, lam, q, kn, vn, norm_g, cache_k, cache_v, *, layer, n_new,
                 out_scale):
    db, n_pages = page_table.shape
    G = DECODE_PAGES_PER_STEP
    while n_pages % G:
        G //= 2
    nj = n_pages // G
    pt_flat = page_table.reshape(-1)

    def page_map(g):
        return lambda b, j, pt: (layer, pt[b * n_pages + j * G + g], 0, 0)

    per_b = lambda a: pl.BlockSpec((1,) + a.shape[1:], lambda b, j, pt: (b, 0, 0))
    page_spec = lambda g: pl.BlockSpec((None, None, PAGE_ROWS, HEAD_W), page_map(g))
    as_rows = lambda c: c.reshape(c.shape[0], c.shape[1], PAGE_ROWS, HEAD_W)
    grid_spec = pltpu.PrefetchScalarGridSpec(
        num_scalar_prefetch=1,
        grid=(db, nj),
        in_specs=[pl.BlockSpec(memory_space=pltpu.SMEM), per_b(q), per_b(kn), per_b(vn),
                  pl.BlockSpec((1, HEAD_W), lambda b, j, pt: (0, 0))]
                 + [page_spec(g) for g in range(G)] + [page_spec(g) for g in range(G)],
        out_specs=pl.BlockSpec((1, Q_ROWS, GROUP_W), lambda b, j, pt: (b, 0, 0)),
        scratch_shapes=[pltpu.VMEM((QS_ROWS, LANES), F32), pltpu.VMEM((QS_ROWS, LANES), F32),
                        pltpu.VMEM((QS_ROWS, HEAD_W), F32)],
    )
    return pl.pallas_call(
        functools.partial(_attn_decode_body, pages_per_step=G, n_new=n_new,
                          out_scale=out_scale),
        grid_spec=grid_spec,
        out_shape=jax.ShapeDtypeStruct((db, Q_ROWS, GROUP_W), F32),
        compiler_params=pltpu.CompilerParams(
            dimension_semantics=("parallel", "arbitrary"), vmem_limit_bytes=VMEM_LIMIT),
        name="attn_decode",
    )(pt_flat, lam, q, kn, vn, norm_g, *([as_rows(cache_k)] * G), *([as_rows(cache_v)] * G))


def _oproj_router_body(xp_ref, xs_ref, gop_ref, gos_ref, dop_ref, dos_ref, wo_ref, ln_ref,
                       wr_ref, br_ref, h_ref, hn_ref, gate_ref, idx_ref, cnt_ref, cnt_scr,
                       *, npt):
    is_p = pl.program_id(0) < npt
    x = jnp.where(is_p, xp_ref[...], xs_ref[...])
    go = jnp.where(is_p, gop_ref[...], gos_ref[...])
    do = jnp.where(is_p, dop_ref[...], dos_ref[...])
    h = (x
         + jnp.dot(go.astype(BF16), wo_ref[0:GROUP_W, :], preferred_element_type=F32)
         + jnp.dot(do.astype(BF16), wo_ref[GROUP_W:, :], preferred_element_type=F32))
    h_ref[...] = h
    hn = h * lax.rsqrt(jnp.mean(h * h, axis=-1, keepdims=True) + EPS) * ln_ref[...]
    hn_ref[...] = hn
    logits = _dot3s(_split2s(hn), _split2s(wr_ref[...])) + br_ref[...]
    lane = lax.broadcasted_iota(I32, logits.shape, 1)
    lane_f = lane.astype(F32)
    cur = jnp.where(lane < N_EXPERTS, logits, -jnp.inf)
    vals, idxs = [], []
    for _ in range(TOP_K):
        m = jnp.max(cur, axis=1, keepdims=True)
        i = jnp.min(jnp.where(cur == m, lane_f, float(LANES)), axis=1, keepdims=True)
        vals.append(m)
        idxs.append(i)
        cur = jnp.where(lane_f == i, -jnp.inf, cur)
    es = [jnp.exp(v - vals[0]) for v in vals]
    den = es[0]
    for e in es[1:]:
        den = den + e
    @pl.when(pl.program_id(0) == 0)
    def _init():
        cnt_scr[...] = jnp.zeros(cnt_scr.shape, F32)

    onehots = [jnp.where(lane_f == i, 1.0, 0.0) for i in idxs]
    chosen = onehots[0]
    for oh in onehots[1:]:
        chosen = chosen + oh
    tm = logits.shape[0]
    r = lax.broadcasted_iota(I32, (tm, tm), 0)
    c = lax.broadcasted_iota(I32, (tm, tm), 1)
    before = jnp.dot(jnp.where(r > c, 1.0, 0.0).astype(BF16), chosen.astype(BF16),
                     preferred_element_type=F32) + cnt_scr[0:1, :]
    gate = jnp.zeros(logits.shape, F32)
    idx = jnp.zeros(logits.shape, F32)
    for k in range(TOP_K):
        gate = jnp.where(lane == k, es[k] / den, gate)
        idx = jnp.where(lane == k, idxs[k], idx)
        rank = jnp.sum(onehots[k] * before, axis=1, keepdims=True)
        idx = jnp.where(lane == TOP_K + k, rank, idx)
    gate_ref[...] = gate
    idx_ref[...] = idx.astype(I32)
    cnt_scr[...] = cnt_scr[...] + jnp.sum(chosen, axis=0, keepdims=True)
    cnt_ref[...] = cnt_scr[...]


def _oproj_router(x_p, x_s, go_p, go_s, do_p, do_s, w_o, ln_g, w_r, b_r):
    n_p, n_s = x_p.shape[0], x_s.shape[0]
    tm = TOKEN_TILE
    npt, nst = n_p // tm, n_s // tm
    n = n_p + n_s
    row = lambda w: pl.BlockSpec((tm, w), lambda i: (i, 0))
    full = lambda a: pl.BlockSpec(a.shape, lambda i: (0,) * a.ndim)
    outs = [(D_MODEL, F32), (D_MODEL, F32), (LANES, F32), (LANES, I32)]
    cnt_spec = pl.BlockSpec((SUBLANES, LANES), lambda i: (0, 0))
    return pl.pallas_call(
        functools.partial(_oproj_router_body, npt=npt),
        grid=(npt + nst,),
        in_specs=[_prompt_spec(D_MODEL, npt), _sample_spec(D_MODEL, npt),
                  _prompt_spec(GROUP_W, npt), _sample_spec(GROUP_W, npt),
                  _prompt_spec(GROUP_W, npt), _sample_spec(GROUP_W, npt),
                  full(w_o), full(ln_g), full(w_r), full(b_r)],
        out_specs=[row(w) for w, _ in outs] + [cnt_spec],
        out_shape=[jax.ShapeDtypeStruct((n, w), dt) for w, dt in outs]
                  + [jax.ShapeDtypeStruct((SUBLANES, LANES), F32)],
        scratch_shapes=[pltpu.VMEM((SUBLANES, LANES), F32)],
        compiler_params=pltpu.CompilerParams(
            dimension_semantics=("arbitrary",), vmem_limit_bytes=VMEM_LIMIT),
        name="oproj_router",
    )(x_p, x_s, go_p, go_s, do_p, do_s, w_o, ln_g, w_r, b_r)


MOE_TILE = 512


def _moe_body(te_ref, nused_ref, xs_ref, wg_ref, wu_ref, wd_ref, bg_ref, bu_ref, bd_ref,
              ys_ref, wg_bf, wu_bf, wd_bf):
    i = pl.program_id(0)
    e = te_ref[i]
    prev = te_ref[jnp.maximum(i - 1, 0)]

    @pl.when((i == 0) | (e != prev))
    def _cast():
        wg_bf[...] = wg_ref[0].astype(BF16)
        wu_bf[...] = wu_ref[0].astype(BF16)
        wd_bf[...] = wd_ref[0].astype(BF16)

    @pl.when(i < nused_ref[0])
    def _compute():
        x = xs_ref[...].astype(BF16)
        gl = jnp.minimum(jnp.dot(x, wg_bf[...], preferred_element_type=F32) + bg_ref[0],
                         SWIGLU_LIMIT)
        lin = jnp.clip(jnp.dot(x, wu_bf[...], preferred_element_type=F32) + bu_ref[0],
                       -SWIGLU_LIMIT, SWIGLU_LIMIT)
        act = gl * _sigmoid(SWIGLU_ALPHA * gl) * (lin + 1.0)
        ys_ref[...] = jnp.dot(act.astype(BF16), wd_bf[...],
                              preferred_element_type=F32) + bd_ref[0]

    @pl.when(i >= nused_ref[0])
    def _idle():
        ys_ref[...] = jnp.zeros(ys_ref.shape, F32)


def _moe_experts(tile_expert, n_used, xs, w_gate, b_gate, w_up, b_up, w_down, b_down):
    r = xs.shape[0]
    tm = MOE_TILE
    d_ff = w_gate.shape[-1]
    wspec = lambda a: pl.BlockSpec((1,) + a.shape[1:], lambda i, te, nu: (te[i], 0, 0))
    grid_spec = pltpu.PrefetchScalarGridSpec(
        num_scalar_prefetch=2,
        grid=(r // tm,),
        in_specs=[pl.BlockSpec((tm, D_MODEL), lambda i, te, nu: (i, 0)),
                  wspec(w_gate), wspec(w_up), wspec(w_down),
                  wspec(b_gate), wspec(b_up), wspec(b_down)],
        out_specs=pl.BlockSpec((tm, D_MODEL), lambda i, te, nu: (i, 0)),
        scratch_shapes=[pltpu.VMEM((D_MODEL, d_ff), BF16), pltpu.VMEM((D_MODEL, d_ff), BF16),
                        pltpu.VMEM((d_ff, D_MODEL), BF16)],
    )
    return pl.pallas_call(
        _moe_body,
        grid_spec=grid_spec,
        out_shape=jax.ShapeDtypeStruct((r, D_MODEL), F32),
        compiler_params=pltpu.CompilerParams(
            dimension_semantics=("arbitrary",), vmem_limit_bytes=VMEM_LIMIT),
        name="moe_experts",
    )(tile_expert, n_used, xs, w_gate, w_up, w_down, b_gate, b_up, b_down)


def _combine_body(h_ref, yg_ref, gate_ref, fg_ref, yp_ref, ys_ref, *, npt):
    gate = gate_ref[...]
    out = h_ref[...]
    for k in range(TOP_K):
        out = out + gate[:, k:k + 1] * yg_ref[k]
    y = out * lax.rsqrt(jnp.mean(out * out, axis=-1, keepdims=True) + EPS) * fg_ref[...]
    i = pl.program_id(0)

    @pl.when(i < npt)
    def _prompt():
        yp_ref[...] = y

    @pl.when(i >= npt)
    def _sample():
        ys_ref[...] = y


def _combine_norm(h, yg, gate, final_g, *, n_p):
    n = h.shape[0]
    tm = TOKEN_TILE
    npt = n_p // tm
    row = lambda w: pl.BlockSpec((tm, w), lambda i: (i, 0))
    return pl.pallas_call(
        functools.partial(_combine_body, npt=npt),
        grid=(n // tm,),
        in_specs=[row(D_MODEL), pl.BlockSpec((TOP_K, tm, D_MODEL), lambda i: (0, i, 0)),
                  row(LANES), pl.BlockSpec((1, D_MODEL), lambda i: (0, 0))],
        out_specs=[_prompt_spec(D_MODEL, npt), _sample_spec(D_MODEL, npt)],
        out_shape=[jax.ShapeDtypeStruct((n_p, D_MODEL), F32),
                   jax.ShapeDtypeStruct((n - n_p, D_MODEL), F32)],
        compiler_params=pltpu.CompilerParams(
            dimension_semantics=("arbitrary",), vmem_limit_bytes=VMEM_LIMIT),
        name="combine_norm",
    )(h, yg, gate, final_g)


def _route(top_i, rank, counts, tile):
    n_assign = top_i.shape[0] * TOP_K
    e_flat = top_i.reshape(-1)
    padded = ((counts + tile - 1) // tile) * tile
    ends = jnp.cumsum(padded)
    starts = ends - padded
    pos = starts[e_flat] + rank.reshape(-1)
    n_pad = N_EXPERTS * tile
    n_tiles = (n_assign + n_pad) // tile
    tile_start = jnp.arange(n_tiles, dtype=I32) * tile
    tile_expert = jnp.minimum(
        jnp.sum((tile_start[:, None] >= ends[None, :]).astype(I32), axis=1), N_EXPERTS - 1)
    n_used = (ends[-1] // tile).astype(I32).reshape(1)
    fill_ends = jnp.cumsum(padded - counts)
    fill_e = jnp.sum((jnp.arange(n_pad, dtype=I32)[:, None] >= fill_ends[None, :]).astype(I32),
                     axis=1)
    order = jnp.argsort(jnp.concatenate([e_flat, fill_e]), stable=True).astype(I32)
    row_token = jnp.where(order < n_assign, order // TOP_K,
                          (order - n_assign) % top_i.shape[0])
    return pos.astype(I32), row_token.astype(I32), tile_expert.astype(I32), n_used


def kernel(x_prompt, x_sample, cache_k, cache_v, state_gdn, state_conv, page_table, ln1_g, w_in, conv_w, a_log, dt_bias, gdn_norm_g, lambda_q1, lambda_k1, lambda_q2, lambda_k2, diff_norm_g, w_o, ln2_g, w_router, b_router, w_gate, b_gate, w_up, b_up, w_down, b_down, final_g):
    B, L, _ = x_prompt.shape
    DB, T, _ = x_sample.shape
    depth = w_in.shape[0]
    assert depth == 1
    l = 0
    n_p, n_s = B * L, DB * T
    lam_init = 0.8 - 0.6 * math.exp(-0.3 * l)
    s1 = jnp.sum(lambda_q1[l].astype(F32) * lambda_k1[l].astype(F32))
    s2 = jnp.sum(lambda_q2[l].astype(F32) * lambda_k2[l].astype(F32))
    lam = (jnp.exp(s1) - jnp.exp(s2) + lam_init).reshape(1, 1).astype(F32)

    x_p = x_prompt.reshape(n_p, D_MODEL)
    x_s = x_sample.reshape(n_s, D_MODEL)
    wl = w_in[l]
    o_z = CONV_DIM
    o_a = o_z + GROUP_W
    o_q = o_a + 2 * N_HEADS
    w_cat = jnp.concatenate(
        [wl[:, :o_a], wl[:, o_q:], wl[:, o_a:o_q],
         jnp.zeros((D_MODEL, LANES - 2 * N_HEADS), wl.dtype)], axis=1).astype(BF16)
    (qkv, z, ab, qdb, kdb, kd_p, kd_s, vd_p, vd_s, qt, vt) = _inproj(
        x_p, x_s, ln1_g[l].reshape(1, D_MODEL), w_cat, batch=B)

    gpar = jnp.zeros((8, LANES), F32)
    gpar = gpar.at[0, :N_HEADS].set(a_log[l].astype(F32)).at[1, :N_HEADS].set(dt_bias[l].astype(F32))
    norm_g = gdn_norm_g[l].reshape(1, HEAD_W).astype(F32)
    cw = conv_w[l].astype(F32)
    *g_outs, s_p = _gdn_stacked(
        qkv, ab, z, jnp.zeros((B, HIST_ROWS, CONV_DIM), F32),
        jnp.zeros((B, N_HEADS, HEAD_W, HEAD_W), F32), cw, gpar, norm_g,
        n_seq=B, l=L, cin=GDN_CHUNK, valid=GDN_CHUNK, nspec=B, spp=1)
    g_out_p = jnp.concatenate(g_outs, axis=0)

    def pad_tok(a):
        return jnp.pad(a.reshape(DB, T, a.shape[-1]), ((0, 0), (0, Q_ROWS - T), (0, 0)))

    def pad_tok2(a):
        return pad_tok(a).reshape(DB * Q_ROWS, a.shape[-1])

    qkv_s = qkv[n_p:]
    hist_s = jnp.pad(state_conv[l].astype(F32), ((0, 0), (HIST_ROWS - (CONV_W - 1), 0), (0, 0)))
    sps = math.gcd(DB, GDN_STACK_ROWS // (N_HEADS * Q_ROWS))
    g_out_s, s_s = _gdn_stacked(
        pad_tok2(qkv_s), pad_tok2(ab[n_p:]), pad_tok2(z[n_p:]), hist_s,
        state_gdn[l].astype(F32), cw, gpar, norm_g,
        n_seq=DB, l=Q_ROWS, cin=Q_ROWS, valid=T, nspec=1, spp=sps)
    g_out_s = g_out_s.reshape(DB, Q_ROWS, GROUP_W)[:, :T].reshape(n_s, GROUP_W)

    dn_g = diff_norm_g[l].reshape(1, HEAD_W).astype(F32)
    d_out_p = _attn_prompt(lam, qt, kdb, vt, dn_g, batch=B, seq=L, tk=TOKEN_TILE,
                           out_scale=1.0 - lam_init)

    d_out_s = _attn_decode(
        page_table, lam, pad_tok(qdb[n_p:]), kd_s.reshape(DB, T * N_HEADS, HEAD_W), vd_s.reshape(DB, T * N_HEADS, HEAD_W), dn_g,
        cache_k, cache_v, layer=l, n_new=T, out_scale=1.0 - lam_init)
    d_out_s = d_out_s[:, :T].reshape(n_s, GROUP_W)

    w_r = jnp.pad(w_router[l].astype(F32), ((0, 0), (0, LANES - N_EXPERTS)))
    b_r = jnp.pad(b_router[l].astype(F32), (0, LANES - N_EXPERTS)).reshape(1, LANES)
    h, hn, gate, idx, cnt = _oproj_router(
        x_p, x_s, g_out_p, g_out_s, d_out_p, d_out_s,
        w_o[l].astype(BF16), ln2_g[l].reshape(1, D_MODEL), w_r, b_r)

    pos, row_token, tile_expert, n_used = _route(
        idx[:, :TOP_K], idx[:, TOP_K:2 * TOP_K], cnt[0, :N_EXPERTS].astype(I32), MOE_TILE)
    xs = hn.at[row_token].get(mode="promise_in_bounds")
    ys = _moe_experts(tile_expert, n_used, xs, w_gate[l], b_gate[l][:, None, :],
                      w_up[l], b_up[l][:, None, :], w_down[l], b_down[l][:, None, :])

    yg = ys.at[pos.reshape(n_p + n_s, TOP_K).T].get(mode="promise_in_bounds")
    y_p, y_s = _combine_norm(h, yg, gate, final_g.reshape(1, D_MODEL).astype(F32), n_p=n_p)

    dt_k, dt_v = cache_k.dtype, cache_v.dtype
    conv_p = jnp.stack([qkv[(i + 1) * L - (CONV_W - 1):(i + 1) * L] for i in range(B)])
    conv_s = qkv_s.reshape(DB, T, CONV_DIM)[:, T - (CONV_W - 1):]
    return (y_p.reshape(B, L, D_MODEL),
            y_s.reshape(DB, T, D_MODEL),
            kd_p.reshape(1, B, L, N_HEADS, HEAD_W).astype(dt_k),
            vd_p.reshape(1, B, L, N_HEADS, HEAD_W).astype(dt_v),
            kd_s.reshape(1, DB, T, N_HEADS, HEAD_W).astype(dt_k),
            vd_s.reshape(1, DB, T, N_HEADS, HEAD_W).astype(dt_v),
            s_p[None].astype(state_gdn.dtype),
            conv_p[None].astype(state_conv.dtype),
            s_s[None].astype(state_gdn.dtype),
            conv_s[None].astype(state_conv.dtype))
```

```python
import functools
import math

import jax
import jax.numpy as jnp
from jax import lax
from jax.experimental import pallas as pl
from jax.experimental.pallas import tpu as pltpu

F32 = jnp.float32
BF16 = jnp.bfloat16
I32 = jnp.int32

D_MODEL = 1024
HEAD_W = 128
N_HEADS = 4
GROUP_W = N_HEADS * HEAD_W
CONV_W = 4
CONV_DIM = 3 * GROUP_W
GDN_CHUNK = 64
GDN_STACK_ROWS = 256
DIFF_HALF = HEAD_W // 2
PAGE = 128
N_EXPERTS = 32
TOP_K = 4
SWIGLU_LIMIT = 7.0
SWIGLU_ALPHA = 1.702
EPS = 1e-6
LANES = 128
SUBLANES = 8
LOG2E = 1.4426950408889634
HIST_ROWS = 8
D_IN_PAD = CONV_DIM + 4 * GROUP_W + LANES
VMEM_LIMIT = 56 * 1024 * 1024


def _pick_tile(n, pref, mult=16):
    t = min(pref, n)
    while t > mult and (n % t or t % mult):
        t -= mult
    assert n % t == 0, (n, pref)
    return t


def _dot(a, b):
    return jnp.dot(a.astype(BF16), b.astype(BF16), preferred_element_type=F32)


def _dot_nt(a, b):
    return lax.dot_general(a.astype(BF16), b.astype(BF16), (((1,), (1,)), ((), ())),
                           preferred_element_type=F32)


def _split2s(x):
    hi = x.astype(BF16)
    return hi, (x - hi.astype(F32)).astype(BF16)


def _dot3s(a, b, dot=None):
    dot = dot or (lambda p, q: jnp.dot(p, q, preferred_element_type=F32))
    return dot(a[0], b[0]) + (dot(a[0], b[1]) + dot(a[1], b[0]))


def _sigmoid(x):
    return 1.0 / (1.0 + jnp.exp(-x))


TOKEN_TILE = 512


def _prompt_spec(w, npt):
    return pl.BlockSpec((TOKEN_TILE, w), lambda i: (jnp.minimum(i, npt - 1), 0))


def _sample_spec(w, npt):
    return pl.BlockSpec((TOKEN_TILE, w), lambda i: (jnp.maximum(i - npt, 0), 0))


def _inproj_body(xp_ref, xs_ref, g_ref, w_ref, qkv_ref, z_ref, ab_ref, qdb_ref, kdb_ref,
                 kdp_ref, kds_ref, vdp_ref, vds_ref, qt_ref, vt_ref, *, npt):
    i = pl.program_id(0)
    x = jnp.where(i < npt, xp_ref[...], xs_ref[...])
    ms = jnp.mean(x * x, axis=-1, keepdims=True)
    xn = (x * lax.rsqrt(ms + EPS) * g_ref[...]).astype(BF16)

    def seg(lo, hi):
        return jnp.dot(xn, w_ref[:, lo:hi], preferred_element_type=F32)

    o = CONV_DIM
    qkv_ref[...] = seg(0, o)
    z_ref[...] = seg(o, o + GROUP_W)
    ab_ref[...] = seg(o + 4 * GROUP_W, o + 4 * GROUP_W + LANES)
    qd = seg(o + GROUP_W, o + 2 * GROUP_W) * (DIFF_HALF ** -0.5 * LOG2E)
    qdb_ref[...] = qd.astype(BF16)
    kd = seg(o + 2 * GROUP_W, o + 3 * GROUP_W)
    kdb_ref[...] = kd.astype(BF16)
    vd = seg(o + 3 * GROUP_W, o + 4 * GROUP_W)

    def store_heads(ref, val):
        for h in range(N_HEADS):
            ref[pl.ds(h, TOKEN_TILE, stride=N_HEADS), :] = val[:, h * HEAD_W:(h + 1) * HEAD_W]

    @pl.when(i < npt)
    def _prompt():
        store_heads(kdp_ref, kd)
        store_heads(vdp_ref, vd)
        qt_ref[...] = qd.T.astype(BF16)
        vt_ref[...] = vd.T.astype(BF16).reshape(N_HEADS, HEAD_W, TOKEN_TILE)

    @pl.when(i >= npt)
    def _sample():
        store_heads(kds_ref, kd)
        store_heads(vds_ref, vd)


def _inproj(x_p, x_s, ln_g, w_cat, *, batch):
    n_p, n_s = x_p.shape[0], x_s.shape[0]
    tm = TOKEN_TILE
    assert n_p % (batch * tm) == 0 and n_s % tm == 0
    npt, nst = n_p // tm, n_s // tm
    nkb = npt // batch
    n = n_p + n_s
    row = lambda w: pl.BlockSpec((tm, w), lambda i: (i, 0))
    full = lambda a: pl.BlockSpec(a.shape, lambda i: (0,) * a.ndim)

    def vt_map(i):
        t = jnp.minimum(i, npt - 1)
        return (t // nkb, 0, t % nkb, 0, 0)

    heads_p = pl.BlockSpec((tm * N_HEADS, HEAD_W), lambda i: (jnp.minimum(i, npt - 1), 0))
    heads_s = pl.BlockSpec((tm * N_HEADS, HEAD_W), lambda i: (jnp.maximum(i - npt, 0), 0))

    sds = jax.ShapeDtypeStruct
    return pl.pallas_call(
        functools.partial(_inproj_body, npt=npt),
        grid=(npt + nst,),
        in_specs=[_prompt_spec(D_MODEL, npt), _sample_spec(D_MODEL, npt), full(ln_g), full(w_cat)],
        out_specs=[row(CONV_DIM), row(GROUP_W), row(LANES), row(GROUP_W), row(GROUP_W),
                   heads_p, heads_s, heads_p, heads_s,
                   pl.BlockSpec((GROUP_W, tm), lambda i: (0, jnp.minimum(i, npt - 1))),
                   pl.BlockSpec((None, N_HEADS, None, HEAD_W, tm), vt_map)],
        out_shape=[sds((n, CONV_DIM), F32), sds((n, GROUP_W), F32), sds((n, LANES), F32),
                   sds((n, GROUP_W), BF16), sds((n, GROUP_W), BF16),
                   sds((n_p * N_HEADS, HEAD_W), F32), sds((n_s * N_HEADS, HEAD_W), F32),
                   sds((n_p * N_HEADS, HEAD_W), F32), sds((n_s * N_HEADS, HEAD_W), F32),
                   sds((GROUP_W, n_p), BF16),
                   sds((batch, N_HEADS, nkb, HEAD_W, tm), BF16)],
        compiler_params=pltpu.CompilerParams(
            dimension_semantics=("arbitrary",), vmem_limit_bytes=VMEM_LIMIT),
        name="inproj",
    )(x_p, x_s, ln_g, w_cat)


def _block_unit_lower_inverse(a, eye, blk):
    t = eye - a
    p = a
    n = 1
    while 2 * n < blk:
        if n == 1:
            ps = _split2s(p)
            p = _dot3s(ps, ps)
            t = t + _dot3s(_split2s(t), _split2s(p))
        else:
            p = _dot(p, p)
            t = t + _dot(t, p)
        n *= 2
    return t


def _gdn_stack_body(*refs, nspec, spp, cin, valid, chains_per_stack):
    qkv_refs, ab_refs, z_refs = refs[:nspec], refs[nspec:2 * nspec], refs[2 * nspec:3 * nspec]
    hist0_ref, s0_ref, convw_ref, gpar_ref, ng_ref = refs[3 * nspec:3 * nspec + 5]
    out_refs = refs[3 * nspec + 5:4 * nspec + 5]
    sfin_ref, s_scr, hist_scr = refs[4 * nspec + 5:]
    nseq = nspec * spp
    blk = cin
    c = pl.program_id(1)

    @pl.when(c == 0)
    def _init():
        s_scr[...] = s0_ref[...]
        hist_scr[...] = hist0_ref[...]

    w = convw_ref[...]
    h0 = HIST_ROWS - (CONV_W - 1)
    lane = lax.broadcasted_iota(I32, (cin, LANES), 1)
    rowi = lax.broadcasted_iota(I32, (cin, LANES), 0)
    acts, gbs, zs = [], [], []
    for s in range(nseq):
        ri, off = s // spp, (s % spp) * cin
        x = qkv_refs[ri][off:off + cin, :]
        xp = jnp.concatenate([hist_scr[s], x], axis=0)
        conv = xp[h0:h0 + cin] * w[0:1]
        for j in range(1, CONV_W):
            conv = conv + xp[h0 + j:h0 + j + cin] * w[j:j + 1]
        hist_scr[s] = x[cin - HIST_ROWS:cin]
        acts.append(conv * _sigmoid(conv))
        t = ab_refs[ri][off:off + cin, :]
        sp_in = t + gpar_ref[1:2, :]
        softplus = jnp.maximum(sp_in, 0.0) + jnp.log(1.0 + jnp.exp(-jnp.abs(sp_in)))
        gb = jnp.where(lane < N_HEADS, -jnp.exp(gpar_ref[0:1, :]) * softplus, _sigmoid(t))
        if valid < cin:
            gb = jnp.where(rowi < valid, gb, 0.0)
        gbs.append(gb)
        zs.append(z_refs[ri][off:off + cin, :])

    R = chains_per_stack * blk
    shift = blk.bit_length() - 1
    r = lax.broadcasted_iota(I32, (R, R), 0)
    q_ = lax.broadcasted_iota(I32, (R, R), 1)
    same = (r >> shift) == (q_ >> shift)
    incl = same & (r >= q_)
    strict = same & (r > q_)
    eye = jnp.where(r == q_, 1.0, 0.0).astype(F32)
    tril = jnp.where(incl, 1.0, 0.0).astype(BF16)
    nrep = max(R // LANES, 1)

    def l2n(t):
        return t * lax.rsqrt(jnp.sum(t * t, axis=-1, keepdims=True) + EPS)

    def tn(p, q):
        return lax.dot_general(p, q, (((0,), (0,)), ((), ())), preferred_element_type=F32)

    all_chains = [(s, h) for s in range(nseq) for h in range(N_HEADS)]
    for st in range(len(all_chains) // chains_per_stack):
        chains = all_chains[st * chains_per_stack:(st + 1) * chains_per_stack]

        def stack(fn):
            return jnp.concatenate([fn(s, h) for s, h in chains], axis=0)

        q = stack(lambda s, h: l2n(acts[s][:, h * HEAD_W:(h + 1) * HEAD_W])) * (HEAD_W ** -0.5)
        k = stack(lambda s, h: l2n(acts[s][:, GROUP_W + h * HEAD_W:GROUP_W + (h + 1) * HEAD_W]))
        v = stack(lambda s, h: acts[s][:, 2 * GROUP_W + h * HEAD_W:2 * GROUP_W + (h + 1) * HEAD_W])
        g = stack(lambda s, h: jnp.broadcast_to(gbs[s][:, h:h + 1], (blk, HEAD_W)))
        beta = stack(lambda s, h: jnp.broadcast_to(
            gbs[s][:, N_HEADS + h:N_HEADS + h + 1], (blk, HEAD_W)))
        g_hi = g.astype(BF16)
        g_r1 = g - g_hi.astype(F32)
        g_mid = g_r1.astype(BF16)
        g_lo = (g_r1 - g_mid.astype(F32)).astype(BF16)
        gam = (jnp.dot(tril, g_hi, preferred_element_type=F32)
               + jnp.dot(tril, g_mid, preferred_element_type=F32)
               + jnp.dot(tril, g_lo, preferred_element_type=F32))
        gcol = _rep_lanes(gam, nrep)[:, :R]
        grow = jnp.broadcast_to(gam.T[0:1, :], (R, R))
        dec = jnp.where(incl, jnp.exp(jnp.minimum(gcol - grow, 0.0)), 0.0)
        eg = jnp.exp(gam)
        kb = k.astype(BF16)
        kk = lax.dot_general(kb, kb, (((1,), (1,)), ((), ())), preferred_element_type=F32)
        bcol = _rep_lanes(beta, nrep)[:, :R]
        a_mat = jnp.where(strict, bcol * kk * dec, 0.0)
        t_inv = _block_unit_lower_inverse(a_mat, eye, blk)
        rhs = jnp.concatenate([beta * v, beta * k * eg], axis=1)
        sol = _dot(t_inv, rhs)
        u, w_ = sol[:, :HEAD_W], sol[:, HEAD_W:]
        ws, qs = [], []
        for ci, (s, h) in enumerate(chains):
            rows = slice(ci * blk, (ci + 1) * blk)
            wq = jnp.concatenate([w_[rows], q[rows]], axis=0)
            wq_s = _dot(wq, s_scr[s, h])
            ws.append(wq_s[:blk])
            qs.append(wq_s[blk:])
        delta = u - jnp.concatenate(ws, axis=0)
        qk = lax.dot_general(q.astype(BF16), kb, (((1,), (1,)), ((), ())),
                             preferred_element_type=F32) * dec
        deltab = delta.astype(BF16)
        o = eg * jnp.concatenate(qs, axis=0) + jnp.dot(qk.astype(BF16), deltab,
                                                       preferred_element_type=F32)
        glast = jnp.concatenate(
            [jnp.broadcast_to(gam[(ci + 1) * blk - 1:(ci + 1) * blk, :], (blk, HEAD_W))
             for ci in range(len(chains))], axis=0)
        ktb = (k * jnp.exp(glast - gam)).astype(BF16)
        on = o * lax.rsqrt(jnp.mean(o * o, axis=-1, keepdims=True) + EPS) * ng_ref[...]
        for ci, (s, h) in enumerate(chains):
            rows = slice(ci * blk, (ci + 1) * blk)
            e_last = jnp.exp(jnp.broadcast_to(gam[(ci + 1) * blk - 1:(ci + 1) * blk, :],
                                              (HEAD_W, HEAD_W)))
            s_scr[s, h] = e_last * s_scr[s, h] + tn(ktb[rows], deltab[rows])
            zh = zs[s][:, h * HEAD_W:(h + 1) * HEAD_W]
            ri, off = s // spp, (s % spp) * cin
            out_refs[ri][off:off + cin, h * HEAD_W:(h + 1) * HEAD_W] = on[rows] * (zh * _sigmoid(zh))

    @pl.when(c == pl.num_programs(1) - 1)
    def _fin():
        sfin_ref[...] = s_scr[...]


def _gdn_stacked(qkv, ab, z, hist0, s0, conv_w, gpar, norm_g, *, n_seq, l, cin, valid, nspec, spp):
    per_step = nspec * spp
    assert l % cin == 0 and n_seq % per_step == 0 and (spp == 1 or l == cin)
    assert cin % HIST_ROWS == 0 and cin & (cin - 1) == 0
    nc = l // cin
    ng = n_seq // per_step
    cps = min(GDN_STACK_ROWS // cin, per_step * N_HEADS)
    assert (per_step * N_HEADS) % cps == 0

    def tok(w, i):
        return pl.BlockSpec((spp * cin, w), lambda g, c: ((g * nspec + i) * nc + c, 0))

    per_g = lambda a: pl.BlockSpec((per_step,) + a.shape[1:],
                                   lambda g, c: (g,) + (0,) * (a.ndim - 1))
    full = lambda a: pl.BlockSpec(a.shape, lambda g, c: (0,) * a.ndim)
    toks = lambda w: [tok(w, i) for i in range(nspec)]
    out_tok = pl.BlockSpec((spp * cin, GROUP_W), lambda g, c: (g * nc + c, 0))
    return pl.pallas_call(
        functools.partial(_gdn_stack_body, nspec=nspec, spp=spp, cin=cin, valid=valid,
                          chains_per_stack=cps),
        grid=(ng, nc),
        in_specs=toks(CONV_DIM) + toks(LANES) + toks(GROUP_W)
                 + [per_g(hist0), per_g(s0), full(conv_w), full(gpar), full(norm_g)],
        out_specs=[out_tok] * nspec + [per_g(s0)],
        out_shape=[jax.ShapeDtypeStruct((ng * spp * l, GROUP_W), F32)] * nspec
                  + [jax.ShapeDtypeStruct(s0.shape, F32)],
        scratch_shapes=[pltpu.VMEM((per_step, N_HEADS, HEAD_W, HEAD_W), F32),
                        pltpu.VMEM((per_step, HIST_ROWS, CONV_DIM), F32)],
        compiler_params=pltpu.CompilerParams(
            dimension_semantics=("parallel", "arbitrary"), vmem_limit_bytes=VMEM_LIMIT),
        name="gdn",
    )(*([qkv] * nspec), *([ab] * nspec), *([z] * nspec), hist0, s0, conv_w, gpar, norm_g)


ATTN_TQ = 1024
ATTN_TK = 1024


def _rep_lanes(x, n):
    return x if n == 1 else jnp.concatenate([x] * n, axis=1)


def _sublane_all(x, op):
    for shift in (4, 2, 1):
        x = op(x, pltpu.roll(x, shift, axis=0))
    return x


def _attn_prompt_body(lam_ref, qt_ref, k_ref, vt_ref, ng_ref, o_ref, m_scr, l_scr, acc_scr,
                      *, tq, tk, out_scale):
    qi = pl.program_id(2)
    qt = qt_ref[...]
    row = lax.broadcasted_iota(I32, qt.shape, 0)
    zero = jnp.zeros_like(qt)
    qqt = jnp.concatenate([jnp.where(row < DIFF_HALF, qt, zero),
                           jnp.where(row >= DIFF_HALF, qt, zero)], axis=1)
    m_scr[...] = jnp.full(m_scr.shape, -jnp.inf, F32)
    l_scr[...] = jnp.zeros(l_scr.shape, F32)
    acc_scr[...] = jnp.zeros(acc_scr.shape, F32)
    nq2 = 2 * tq

    def block(kb, masked):
        start = pl.multiple_of(kb * tk, tk)
        k = k_ref[pl.ds(start, tk), :]
        s = jnp.dot(k, qqt, preferred_element_type=F32)
        if masked:
            key = lax.broadcasted_iota(I32, s.shape, 0) + (kb * tk - qi * tq)
            qc = lax.broadcasted_iota(I32, s.shape, 1)
            qc = jnp.where(qc >= tq, qc - tq, qc)
            s = jnp.where(key <= qc, s, -jnp.inf)
        s3 = s.reshape(tk // SUBLANES, SUBLANES, nq2)
        m_prev = m_scr[...]
        m_new = jnp.maximum(m_prev, _sublane_all(jnp.max(s3, axis=0), jnp.maximum))
        alpha = jnp.exp2(m_prev - m_new)
        p3 = jnp.exp2(s3 - m_new[None])
        l_scr[...] = alpha * l_scr[...] + _sublane_all(jnp.sum(p3, axis=0), jnp.add)
        p = p3.reshape(tk, nq2).astype(BF16)
        nsub = tk // TOKEN_TILE
        pv = jnp.dot(vt_ref[kb * nsub], p[:TOKEN_TILE], preferred_element_type=F32)
        for j in range(1, nsub):
            pv = pv + jnp.dot(vt_ref[kb * nsub + j], p[j * TOKEN_TILE:(j + 1) * TOKEN_TILE],
                              preferred_element_type=F32)
        acc = acc_scr[...].reshape(HEAD_W // SUBLANES, SUBLANES, nq2) * alpha[None]
        acc_scr[...] = acc.reshape(HEAD_W, nq2) + pv
        m_scr[...] = m_new

    def body(kb, carry):
        block(kb, False)
        return carry

    def body_masked(kb, carry):
        block(kb, True)
        return carry

    n_full = (qi * tq) // tk
    n_all = (qi * tq + tq + tk - 1) // tk
    lax.fori_loop(0, n_full, body, 0)
    lax.fori_loop(n_full, n_all, body_masked, 0)

    ot = acc_scr[...] / l_scr[0:1, :]
    ot = ot[:, :tq] - lam_ref[0, 0] * ot[:, tq:]
    o = ot.T
    o = o * lax.rsqrt(jnp.mean(o * o, axis=-1, keepdims=True) + EPS) * ng_ref[...]
    o_ref[...] = o * out_scale


def _attn_prompt(lam, qt, kdb, vt, norm_g, *, batch, seq, tk, out_scale):
    tq = _pick_tile(seq, ATTN_TQ, LANES)
    nq = seq // tq
    return pl.pallas_call(
        functools.partial(_attn_prompt_body, tq=tq, tk=tk, out_scale=out_scale),
        grid=(batch, N_HEADS, nq),
        in_specs=[pl.BlockSpec(memory_space=pltpu.SMEM),
                  pl.BlockSpec((HEAD_W, tq), lambda b, h, i: (h, b * nq + i)),
                  pl.BlockSpec((seq, HEAD_W), lambda b, h, i: (b, h)),
                  pl.BlockSpec((None, None, seq // TOKEN_TILE, HEAD_W, TOKEN_TILE),
                               lambda b, h, i: (b, h, 0, 0, 0)),
                  pl.BlockSpec((1, HEAD_W), lambda b, h, i: (0, 0))],
        out_specs=pl.BlockSpec((tq, HEAD_W), lambda b, h, i: (b * nq + i, h)),
        out_shape=jax.ShapeDtypeStruct((batch * seq, GROUP_W), F32),
        scratch_shapes=[pltpu.VMEM((SUBLANES, 2 * tq), F32), pltpu.VMEM((SUBLANES, 2 * tq), F32),
                        pltpu.VMEM((HEAD_W, 2 * tq), F32)],
        compiler_params=pltpu.CompilerParams(
            dimension_semantics=("parallel", "parallel", "arbitrary"),
            vmem_limit_bytes=VMEM_LIMIT),
        name="attn_prompt",
    )(lam, qt, kdb, vt, norm_g)


Q_ROWS = 8
QH_ROWS = 2 * Q_ROWS
QS_ROWS = N_HEADS * QH_ROWS
DECODE_PAGES_PER_STEP = 32
PAGE_ROWS = PAGE * N_HEADS


def _attn_decode_body(pt_ref, lam_ref, q_ref, kn_ref, vn_ref, ng_ref, *rest,
                      pages_per_step, n_new, out_scale):
    G = pages_per_step
    k_refs, v_refs = rest[:G], rest[G:2 * G]
    o_ref, m_scr, l_scr, acc_scr = rest[2 * G:]
    j = pl.program_id(1)

    @pl.when(j == 0)
    def _init():
        m_scr[...] = jnp.full(m_scr.shape, -jnp.inf, F32)
        l_scr[...] = jnp.zeros(l_scr.shape, F32)
        acc_scr[...] = jnp.zeros(acc_scr.shape, F32)

    q = q_ref[0]
    lane = lax.broadcasted_iota(I32, (Q_ROWS, HEAD_W), 1)
    zero = jnp.zeros((Q_ROWS, HEAD_W), BF16)
    qs = []
    for h in range(N_HEADS):
        t = q[:, h * HEAD_W:(h + 1) * HEAD_W]
        qs += [jnp.where(lane < DIFF_HALF, t, zero), jnp.where(lane >= DIFF_HALF, t, zero)]
    qs = jnp.concatenate(qs, axis=0)

    def update(ks, vs, masks):
        s = jnp.concatenate([jnp.where(mk, _dot_nt(qs, k), -jnp.inf)
                             for k, mk in zip(ks, masks)], axis=1)
        m_prev = m_scr[...]
        m_new = jnp.maximum(m_prev, jnp.max(s, axis=1, keepdims=True))
        alpha = jnp.exp2(m_prev - m_new)
        p = jnp.exp2(s - _rep_lanes(m_new, s.shape[1] // LANES))
        l_scr[...] = alpha * l_scr[...] + jnp.sum(p, axis=1, keepdims=True)
        p = p.astype(BF16)
        pv, c0 = None, 0
        for v in vs:
            d = jnp.dot(p[:, c0:c0 + v.shape[0]], v, preferred_element_type=F32)
            pv = d if pv is None else pv + d
            c0 += v.shape[0]
        acc_scr[...] = alpha * acc_scr[...] + pv
        m_scr[...] = m_new

    def head_masks(rows):
        r = lax.broadcasted_iota(I32, (QS_ROWS, rows), 0)
        c = lax.broadcasted_iota(I32, (QS_ROWS, rows), 1)
        return r, c, (c & (N_HEADS - 1)) == (r >> (QH_ROWS.bit_length() - 1))

    same_head = head_masks(PAGE_ROWS)[2]
    update([k_refs[g][...].astype(BF16) for g in range(G)],
           [v_refs[g][...].astype(BF16) for g in range(G)], [same_head] * G)

    @pl.when(j == pl.num_programs(1) - 1)
    def _fin():
        pad = jnp.zeros((LANES - kn_ref.shape[1], HEAD_W), F32)
        kn = jnp.concatenate([kn_ref[0], pad], axis=0).astype(BF16)
        vn = jnp.concatenate([vn_ref[0], pad], axis=0).astype(BF16)
        r, c, same = head_masks(LANES)
        tok = c >> (N_HEADS.bit_length() - 1)
        update([kn], [vn], [same & (tok <= (r & (Q_ROWS - 1))) & (tok < n_new)])
        o = acc_scr[...] / l_scr[...]
        lam = lam_ref[0, 0]
        for h in range(N_HEADS):
            r0 = h * QH_ROWS
            oh = o[r0:r0 + Q_ROWS] - lam * o[r0 + Q_ROWS:r0 + QH_ROWS]
            oh = oh * lax.rsqrt(jnp.mean(oh * oh, axis=-1, keepdims=True) + EPS) * ng_ref[...]
            o_ref[0, :, h * HEAD_W:(h + 1) * HEAD_W] = oh * out_scale


def _attn_decode(page_table, lam, q, kn, vn, norm_g, cache_k, cache_v, *, layer, n_new,
                 out_scale):
    db, n_pages = page_table.shape
    G = DECODE_PAGES_PER_STEP
    while n_pages % G:
        G //= 2
    nj = n_pages // G
    pt_flat = page_table.reshape(-1)

    def page_map(g):
        return lambda b, j, pt: (layer, pt[b * n_pages + j * G + g], 0, 0)

    per_b = lambda a: pl.BlockSpec((1,) + a.shape[1:], lambda b, j, pt: (b, 0, 0))
    page_spec = lambda g: pl.BlockSpec((None, None, PAGE_ROWS, HEAD_W), page_map(g))
    as_rows = lambda c: c.reshape(c.shape[0], c.shape[1], PAGE_ROWS, HEAD_W)
    grid_spec = pltpu.PrefetchScalarGridSpec(
        num_scalar_prefetch=1,
        grid=(db, nj),
        in_specs=[pl.BlockSpec(memory_space=pltpu.SMEM), per_b(q), per_b(kn), per_b(vn),
                  pl.BlockSpec((1, HEAD_W), lambda b, j, pt: (0, 0))]
                 + [page_spec(g) for g in range(G)] + [page_spec(g) for g in range(G)],
        out_specs=pl.BlockSpec((1, Q_ROWS, GROUP_W), lambda b, j, pt: (b, 0, 0)),
        scratch_shapes=[pltpu.VMEM((QS_ROWS, LANES), F32), pltpu.VMEM((QS_ROWS, LANES), F32),
                        pltpu.VMEM((QS_ROWS, HEAD_W), F32)],
    )
    return pl.pallas_call(
        functools.partial(_attn_decode_body, pages_per_step=G, n_new=n_new,
                          out_scale=out_scale),
        grid_spec=grid_spec,
        out_shape=jax.ShapeDtypeStruct((db, Q_ROWS, GROUP_W), F32),
        compiler_params=pltpu.CompilerParams(
            dimension_semantics=("parallel", "arbitrary"), vmem_limit_bytes=VMEM_LIMIT),
        name="attn_decode",
    )(pt_flat, lam, q, kn, vn, norm_g, *([as_rows(cache_k)] * G), *([as_rows(cache_v)] * G))


def _oproj_router_body(xp_ref, xs_ref, gop_ref, gos_ref, dop_ref, dos_ref, wo_ref, ln_ref,
                       wr_ref, br_ref, h_ref, hn_ref, gate_ref, idx_ref, cnt_ref, cnt_scr,
                       *, npt):
    is_p = pl.program_id(0) < npt
    x = jnp.where(is_p, xp_ref[...], xs_ref[...])
    go = jnp.where(is_p, gop_ref[...], gos_ref[...])
    do = jnp.where(is_p, dop_ref[...], dos_ref[...])
    h = (x
         + jnp.dot(go.astype(BF16), wo_ref[0:GROUP_W, :], preferred_element_type=F32)
         + jnp.dot(do.astype(BF16), wo_ref[GROUP_W:, :], preferred_element_type=F32))
    h_ref[...] = h
    hn = h * lax.rsqrt(jnp.mean(h * h, axis=-1, keepdims=True) + EPS) * ln_ref[...]
    hn_ref[...] = hn
    logits = _dot3s(_split2s(hn), _split2s(wr_ref[...])) + br_ref[...]
    lane = lax.broadcasted_iota(I32, logits.shape, 1)
    lane_f = lane.astype(F32)
    cur = jnp.where(lane < N_EXPERTS, logits, -jnp.inf)
    vals, idxs = [], []
    for _ in range(TOP_K):
        m = jnp.max(cur, axis=1, keepdims=True)
        i = jnp.min(jnp.where(cur == m, lane_f, float(LANES)), axis=1, keepdims=True)
        vals.append(m)
        idxs.append(i)
        cur = jnp.where(lane_f == i, -jnp.inf, cur)
    es = [jnp.exp(v - vals[0]) for v in vals]
    den = es[0]
    for e in es[1:]:
        den = den + e
    @pl.when(pl.program_id(0) == 0)
    def _init():
        cnt_scr[...] = jnp.zeros(cnt_scr.shape, F32)

    onehots = [jnp.where(lane_f == i, 1.0, 0.0) for i in idxs]
    chosen = onehots[0]
    for oh in onehots[1:]:
        chosen = chosen + oh
    tm = logits.shape[0]
    r = lax.broadcasted_iota(I32, (tm, tm), 0)
    c = lax.broadcasted_iota(I32, (tm, tm), 1)
    before = jnp.dot(jnp.where(r > c, 1.0, 0.0).astype(BF16), chosen.astype(BF16),
                     preferred_element_type=F32) + cnt_scr[0:1, :]
    gate = jnp.zeros(logits.shape, F32)
    idx = jnp.zeros(logits.shape, F32)
    for k in range(TOP_K):
        gate = jnp.where(lane == k, es[k] / den, gate)
        idx = jnp.where(lane == k, idxs[k], idx)
        rank = jnp.sum(onehots[k] * before, axis=1, keepdims=True)
        idx = jnp.where(lane == TOP_K + k, rank, idx)
    gate_ref[...] = gate
    idx_ref[...] = idx.astype(I32)
    cnt_scr[...] = cnt_scr[...] + jnp.sum(chosen, axis=0, keepdims=True)
    cnt_ref[...] = cnt_scr[...]


def _oproj_router(x_p, x_s, go_p, go_s, do_p, do_s, w_o, ln_g, w_r, b_r):
    n_p, n_s = x_p.shape[0], x_s.shape[0]
    tm = TOKEN_TILE
    npt, nst = n_p // tm, n_s // tm
    n = n_p + n_s
    row = lambda w: pl.BlockSpec((tm, w), lambda i: (i, 0))
    full = lambda a: pl.BlockSpec(a.shape, lambda i: (0,) * a.ndim)
    outs = [(D_MODEL, F32), (D_MODEL, F32), (LANES, F32), (LANES, I32)]
    cnt_spec = pl.BlockSpec((SUBLANES, LANES), lambda i: (0, 0))
    return pl.pallas_call(
        functools.partial(_oproj_router_body, npt=npt),
        grid=(npt + nst,),
        in_specs=[_prompt_spec(D_MODEL, npt), _sample_spec(D_MODEL, npt),
                  _prompt_spec(GROUP_W, npt), _sample_spec(GROUP_W, npt),
                  _prompt_spec(GROUP_W, npt), _sample_spec(GROUP_W, npt),
                  full(w_o), full(ln_g), full(w_r), full(b_r)],
        out_specs=[row(w) for w, _ in outs] + [cnt_spec],
        out_shape=[jax.ShapeDtypeStruct((n, w), dt) for w, dt in outs]
                  + [jax.ShapeDtypeStruct((SUBLANES, LANES), F32)],
        scratch_shapes=[pltpu.VMEM((SUBLANES, LANES), F32)],
        compiler_params=pltpu.CompilerParams(
            dimension_semantics=("arbitrary",), vmem_limit_bytes=VMEM_LIMIT),
        name="oproj_router",
    )(x_p, x_s, go_p, go_s, do_p, do_s, w_o, ln_g, w_r, b_r)


MOE_TILE = 512


def _moe_body(te_ref, nused_ref, xs_ref, wg_ref, wu_ref, wd_ref, bg_ref, bu_ref, bd_ref,
              ys_ref, wg_bf, wu_bf, wd_bf):
    i = pl.program_id(0)
    e = te_ref[i]
    prev = te_ref[jnp.maximum(i - 1, 0)]

    @pl.when((i == 0) | (e != prev))
    def _cast():
        wg_bf[...] = wg_ref[0].astype(BF16)
        wu_bf[...] = wu_ref[0].astype(BF16)
        wd_bf[...] = wd_ref[0].astype(BF16)

    @pl.when(i < nused_ref[0])
    def _compute():
        x = xs_ref[...].astype(BF16)
        gl = jnp.minimum(jnp.dot(x, wg_bf[...], preferred_element_type=F32) + bg_ref[0],
                         SWIGLU_LIMIT)
        lin = jnp.clip(jnp.dot(x, wu_bf[...], preferred_element_type=F32) + bu_ref[0],
                       -SWIGLU_LIMIT, SWIGLU_LIMIT)
        act = gl * _sigmoid(SWIGLU_ALPHA * gl) * (lin + 1.0)
        ys_ref[...] = jnp.dot(act.astype(BF16), wd_bf[...],
                              preferred_element_type=F32) + bd_ref[0]

    @pl.when(i >= nused_ref[0])
    def _idle():
        ys_ref[...] = jnp.zeros(ys_ref.shape, F32)


def _moe_experts(tile_expert, n_used, xs, w_gate, b_gate, w_up, b_up, w_down, b_down):
    r = xs.shape[0]
    tm = MOE_TILE
    d_ff = w_gate.shape[-1]
    wspec = lambda a: pl.BlockSpec((1,) + a.shape[1:], lambda i, te, nu: (te[i], 0, 0))
    grid_spec = pltpu.PrefetchScalarGridSpec(
        num_scalar_prefetch=2,
        grid=(r // tm,),
        in_specs=[pl.BlockSpec((tm, D_MODEL), lambda i, te, nu: (i, 0)),
                  wspec(w_gate), wspec(w_up), wspec(w_down),
                  wspec(b_gate), wspec(b_up), wspec(b_down)],
        out_specs=pl.BlockSpec((tm, D_MODEL), lambda i, te, nu: (i, 0)),
        scratch_shapes=[pltpu.VMEM((D_MODEL, d_ff), BF16), pltpu.VMEM((D_MODEL, d_ff), BF16),
                        pltpu.VMEM((d_ff, D_MODEL), BF16)],
    )
    return pl.pallas_call(
        _moe_body,
        grid_spec=grid_spec,
        out_shape=jax.ShapeDtypeStruct((r, D_MODEL), F32),
        compiler_params=pltpu.CompilerParams(
            dimension_semantics=("arbitrary",), vmem_limit_bytes=VMEM_LIMIT),
        name="moe_experts",
    )(tile_expert, n_used, xs, w_gate, w_up, w_down, b_gate, b_up, b_down)


def _combine_body(h_ref, yg_ref, gate_ref, fg_ref, yp_ref, ys_ref, *, npt):
    gate = gate_ref[...]
    out = h_ref[...]
    for k in range(TOP_K):
        out = out + gate[:, k:k + 1] * yg_ref[k]
    y = out * lax.rsqrt(jnp.mean(out * out, axis=-1, keepdims=True) + EPS) * fg_ref[...]
    i = pl.program_id(0)

    @pl.when(i < npt)
    def _prompt():
        yp_ref[...] = y

    @pl.when(i >= npt)
    def _sample():
        ys_ref[...] = y


def _combine_norm(h, yg, gate, final_g, *, n_p):
    n = h.shape[0]
    tm = TOKEN_TILE
    npt = n_p // tm
    row = lambda w: pl.BlockSpec((tm, w), lambda i: (i, 0))
    return pl.pallas_call(
        functools.partial(_combine_body, npt=npt),
        grid=(n // tm,),
        in_specs=[row(D_MODEL), pl.BlockSpec((TOP_K, tm, D_MODEL), lambda i: (0, i, 0)),
                  row(LANES), pl.BlockSpec((1, D_MODEL), lambda i: (0, 0))],
        out_specs=[_prompt_spec(D_MODEL, npt), _sample_spec(D_MODEL, npt)],
        out_shape=[jax.ShapeDtypeStruct((n_p, D_MODEL), F32),
                   jax.ShapeDtypeStruct((n - n_p, D_MODEL), F32)],
        compiler_params=pltpu.CompilerParams(
            dimension_semantics=("arbitrary",), vmem_limit_bytes=VMEM_LIMIT),
        name="combine_norm",
    )(h, yg, gate, final_g)


def _route(top_i, rank, counts, tile):
    n_assign = top_i.shape[0] * TOP_K
    e_flat = top_i.reshape(-1)
    padded = ((counts + tile - 1) // tile) * tile
    ends = jnp.cumsum(padded)
    starts = ends - padded
    pos = starts[e_flat] + rank.reshape(-1)
    n_pad = N_EXPERTS * tile
    n_tiles = (n_assign + n_pad) // tile
    tile_start = jnp.arange(n_tiles, dtype=I32) * tile
    tile_expert = jnp.minimum(
        jnp.sum((tile_start[:, None] >= ends[None, :]).astype(I32), axis=1), N_EXPERTS - 1)
    n_used = (ends[-1] // tile).astype(I32).reshape(1)
    fill_ends = jnp.cumsum(padded - counts)
    fill_e = jnp.sum((jnp.arange(n_pad, dtype=I32)[:, None] >= fill_ends[None, :]).astype(I32),
                     axis=1)
    order = jnp.argsort(jnp.concatenate([e_flat, fill_e]), stable=True).astype(I32)
    row_token = jnp.where(order < n_assign, order // TOP_K,
                          jnp.minimum(order - n_assign, top_i.shape[0] - 1))
    return pos.astype(I32), row_token.astype(I32), tile_expert.astype(I32), n_used


def kernel(x_prompt, x_sample, cache_k, cache_v, state_gdn, state_conv, page_table, ln1_g, w_in, conv_w, a_log, dt_bias, gdn_norm_g, lambda_q1, lambda_k1, lambda_q2, lambda_k2, diff_norm_g, w_o, ln2_g, w_router, b_router, w_gate, b_gate, w_up, b_up, w_down, b_down, final_g):
    B, L, _ = x_prompt.shape
    DB, T, _ = x_sample.shape
    depth = w_in.shape[0]
    assert depth == 1
    l = 0
    n_p, n_s = B * L, DB * T
    lam_init = 0.8 - 0.6 * math.exp(-0.3 * l)
    s1 = jnp.sum(lambda_q1[l].astype(F32) * lambda_k1[l].astype(F32))
    s2 = jnp.sum(lambda_q2[l].astype(F32) * lambda_k2[l].astype(F32))
    lam = (jnp.exp(s1) - jnp.exp(s2) + lam_init).reshape(1, 1).astype(F32)

    x_p = x_prompt.reshape(n_p, D_MODEL)
    x_s = x_sample.reshape(n_s, D_MODEL)
    wl = w_in[l]
    o_z = CONV_DIM
    o_a = o_z + GROUP_W
    o_q = o_a + 2 * N_HEADS
    w_cat = jnp.concatenate(
        [wl[:, :o_a], wl[:, o_q:], wl[:, o_a:o_q],
         jnp.zeros((D_MODEL, LANES - 2 * N_HEADS), wl.dtype)], axis=1).astype(BF16)
    (qkv, z, ab, qdb, kdb, kd_p, kd_s, vd_p, vd_s, qt, vt) = _inproj(
        x_p, x_s, ln1_g[l].reshape(1, D_MODEL), w_cat, batch=B)

    gpar = jnp.zeros((8, LANES), F32)
    gpar = gpar.at[0, :N_HEADS].set(a_log[l].astype(F32)).at[1, :N_HEADS].set(dt_bias[l].astype(F32))
    norm_g = gdn_norm_g[l].reshape(1, HEAD_W).astype(F32)
    cw = conv_w[l].astype(F32)
    *g_outs, s_p = _gdn_stacked(
        qkv, ab, z, jnp.zeros((B, HIST_ROWS, CONV_DIM), F32),
        jnp.zeros((B, N_HEADS, HEAD_W, HEAD_W), F32), cw, gpar, norm_g,
        n_seq=B, l=L, cin=GDN_CHUNK, valid=GDN_CHUNK, nspec=B, spp=1)
    g_out_p = jnp.concatenate(g_outs, axis=0)

    def pad_tok(a):
        return jnp.pad(a.reshape(DB, T, a.shape[-1]), ((0, 0), (0, Q_ROWS - T), (0, 0)))

    def pad_tok2(a):
        return pad_tok(a).reshape(DB * Q_ROWS, a.shape[-1])

    qkv_s = qkv[n_p:]
    hist_s = jnp.pad(state_conv[l].astype(F32), ((0, 0), (HIST_ROWS - (CONV_W - 1), 0), (0, 0)))
    sps = math.gcd(DB, GDN_STACK_ROWS // (N_HEADS * Q_ROWS))
    g_out_s, s_s = _gdn_stacked(
        pad_tok2(qkv_s), pad_tok2(ab[n_p:]), pad_tok2(z[n_p:]), hist_s,
        state_gdn[l].astype(F32), cw, gpar, norm_g,
        n_seq=DB, l=Q_ROWS, cin=Q_ROWS, valid=T, nspec=1, spp=sps)
    g_out_s = g_out_s.reshape(DB, Q_ROWS, GROUP_W)[:, :T].reshape(n_s, GROUP_W)

    dn_g = diff_norm_g[l].reshape(1, HEAD_W).astype(F32)
    d_out_p = _attn_prompt(lam, qt, kdb, vt, dn_g, batch=B, seq=L, tk=ATTN_TK,
                           out_scale=1.0 - lam_init)

    d_out_s = _attn_decode(
        page_table, lam, pad_tok(qdb[n_p:]), kd_s.reshape(DB, T * N_HEADS, HEAD_W), vd_s.reshape(DB, T * N_HEADS, HEAD_W), dn_g,
        cache_k, cache_v, layer=l, n_new=T, out_scale=1.0 - lam_init)
    d_out_s = d_out_s[:, :T].reshape(n_s, GROUP_W)

    w_r = jnp.pad(w_router[l].astype(F32), ((0, 0), (0, LANES - N_EXPERTS)))
    b_r = jnp.pad(b_router[l].astype(F32), (0, LANES - N_EXPERTS)).reshape(1, LANES)
    h, hn, gate, idx, cnt = _oproj_router(
        x_p, x_s, g_out_p, g_out_s, d_out_p, d_out_s,
        w_o[l].astype(BF16), ln2_g[l].reshape(1, D_MODEL), w_r, b_r)

    pos, row_token, tile_expert, n_used = _route(
        idx[:, :TOP_K], idx[:, TOP_K:2 * TOP_K], cnt[0, :N_EXPERTS].astype(I32), MOE_TILE)
    xs = hn.at[row_token].get(mode="promise_in_bounds")
    ys = _moe_experts(tile_expert, n_used, xs, w_gate[l], b_gate[l][:, None, :],
                      w_up[l], b_up[l][:, None, :], w_down[l], b_down[l][:, None, :])

    yg = ys.at[pos.reshape(n_p + n_s, TOP_K).T].get(mode="promise_in_bounds")
    y_p, y_s = _combine_norm(h, yg, gate, final_g.reshape(1, D_MODEL).astype(F32), n_p=n_p)

    dt_k, dt_v = cache_k.dtype, cache_v.dtype
    conv_p = jnp.stack([qkv[(i + 1) * L - (CONV_W - 1):(i + 1) * L] for i in range(B)])
    conv_s = qkv_s.reshape(DB, T, CONV_DIM)[:, T - (CONV_W - 1):]
    return (y_p.reshape(B, L, D_MODEL),
            y_s.reshape(DB, T, D_MODEL),
            kd_p.reshape(1, B, L, N_HEADS, HEAD_W).astype(dt_k),
            vd_p.reshape(1, B, L, N_HEADS, HEAD_W).astype(dt_v),
            kd_s.reshape(1, DB, T, N_HEADS, HEAD_W).astype(dt_k),
            vd_s.reshape(1, DB, T, N_HEADS, HEAD_W).astype(dt_v),
            s_p[None].astype(state_gdn.dtype),
            conv_p[None].astype(state_conv.dtype),
            s_s[None].astype(state_gdn.dtype),
            conv_s[None].astype(state_conv.dtype))
```
